```python
import math
import jax, jax.numpy as jnp
from jax import lax
import numpy as np

D_MODEL = 2048
BATCH = 4
SEQ = 4096
DEPTH = 4

N_MEM = 256
SSD_EXPAND = 2
SSD_D_INNER = SSD_EXPAND * D_MODEL
SSD_HEAD_DIM = 64
SSD_N_HEADS = SSD_D_INNER // SSD_HEAD_DIM
SSD_N_GROUPS = 8
SSD_HEADS_PER_GROUP = SSD_N_HEADS // SSD_N_GROUPS
SSD_D_STATE = 128
SSD_CONV = 4
SSD_CHUNK = 128
SSD_CONV_DIM = SSD_D_INNER + 2 * SSD_N_GROUPS * SSD_D_STATE
ATT_HEAD_DIM = 128
ATT_N_HEADS = D_MODEL // ATT_HEAD_DIM
ATT_N_KV = 4
ATT_GROUP = ATT_N_HEADS // ATT_N_KV
IDX_N_HEADS = 16
IDX_HEAD_DIM = 64
TOPK_MAX = 256
Q_BLOCK = 128
ROPE_THETA = 500000.0
ROPE_FRACTION = 4
MEM_HEADS = 4
MEM_HEAD_DIM = D_MODEL // MEM_HEADS
D_FF = ((8 * D_MODEL + 3 * 256 - 1) // (3 * 256)) * 256
NORM_EPS = 1e-6

IN_PROJ_SIZES = (
    SSD_D_INNER,
    SSD_CONV_DIM,
    SSD_N_HEADS,
    ATT_N_HEADS * ATT_HEAD_DIM,
    ATT_N_KV * ATT_HEAD_DIM,
    ATT_N_KV * ATT_HEAD_DIM,
    IDX_N_HEADS * IDX_HEAD_DIM,
    IDX_HEAD_DIM,
    IDX_N_HEADS,
    D_MODEL,
    D_MODEL,
)
D_IN_PROJ = sum(IN_PROJ_SIZES)

kernel_name = "hybrid_ssd_dsa_gated_trunk"


def rmsnorm(x, g):
    xf = x.astype(jnp.float32)
    y = xf * lax.rsqrt(jnp.mean(xf * xf, axis=-1, keepdims=True) + NORM_EPS)
    return (y * g.astype(jnp.float32)).astype(x.dtype)


def rope_partial(x, positions):
    d = x.shape[-1]
    rot = d // ROPE_FRACTION
    half = rot // 2
    inv_freq = jnp.power(ROPE_THETA, -jnp.arange(half, dtype=jnp.float32) * 2.0 / rot)
    ang = positions.astype(jnp.float32)[..., None] * inv_freq
    cos = jnp.cos(ang)[:, :, None, :]
    sin = jnp.sin(ang)[:, :, None, :]
    xf = x.astype(jnp.float32)
    x1, x2, rest = xf[..., :half], xf[..., half:rot], xf[..., rot:]
    out = jnp.concatenate([x1 * cos - x2 * sin, x2 * cos + x1 * sin, rest], axis=-1)
    return out.astype(x.dtype)


def causal_depthwise_conv(u, w, b):
    c = u.shape[-1]
    y = lax.conv_general_dilated(
        u, w[:, None, :].astype(u.dtype), window_strides=(1,),
        padding=[(SSD_CONV - 1, 0)],
        dimension_numbers=('NWC', 'WIO', 'NWC'), feature_group_count=c)
    return y + b


def ssd_mixer(z, xbc, dt_raw, conv_w, conv_b, dt_bias, a_log, d_skip, norm_g):
    bsz, s_len, _ = z.shape
    G, Hg, P, N, Q = SSD_N_GROUPS, SSD_HEADS_PER_GROUP, SSD_HEAD_DIM, SSD_D_STATE, SSD_CHUNK
    xbc = jax.nn.silu(causal_depthwise_conv(xbc, conv_w, conv_b))
    xs = xbc[..., :SSD_D_INNER].reshape(bsz, s_len, G, Hg, P)
    bm = xbc[..., SSD_D_INNER:SSD_D_INNER + G * N].reshape(bsz, s_len, G, N)
    cm = xbc[..., SSD_D_INNER + G * N:].reshape(bsz, s_len, G, N)
    dt = jax.nn.softplus(dt_raw.astype(jnp.float32) + dt_bias.astype(jnp.float32))
    dt = dt.reshape(bsz, s_len, G, Hg)
    a = -jnp.exp(a_log.astype(jnp.float32)).reshape(G, Hg)
    da = dt * a
    xdt = xs.astype(jnp.float32) * dt[..., None]

    nc = s_len // Q

    def to_chunks(t):
        return jnp.moveaxis(t.reshape((bsz, nc, Q) + t.shape[2:]), 1, 0)

    causal = jnp.tril(jnp.ones((Q, Q), dtype=bool))[None, :, :, None, None]

    def chunk_step(state, inp):
        xdt_c, b_c, c_c, da_c = inp
        b_c = b_c.astype(jnp.float32)
        c_c = c_c.astype(jnp.float32)
        cs = jnp.cumsum(da_c, axis=1)
        diff = cs[:, :, None] - cs[:, None, :]
        decay = jnp.exp(jnp.where(causal, diff, -jnp.inf))
        cb = jnp.einsum('btgn,bsgn->btsg', c_c, b_c)
        y = jnp.einsum('btsg,btsgh,bsghp->btghp', cb, decay, xdt_c)
        y = y + jnp.einsum('btgn,bghpn->btghp', c_c, state) * jnp.exp(cs)[..., None]
        last = cs[:, -1]
        w_end = jnp.exp(last[:, None] - cs)
        state = state * jnp.exp(last)[..., None, None] + jnp.einsum(
            'bsgn,bsghp->bghpn', b_c, xdt_c * w_end[..., None])
        return state, y

    state0 = jnp.zeros((bsz, G, Hg, P, N), dtype=jnp.float32)
    _, ys = lax.scan(chunk_step, state0,
                     (to_chunks(xdt), to_chunks(bm), to_chunks(cm), to_chunks(da)))
    y = jnp.moveaxis(ys, 0, 1).reshape(bsz, s_len, G, Hg, P)
    y = y + d_skip.astype(jnp.float32).reshape(G, Hg)[..., None] * xs.astype(jnp.float32)
    y = y.reshape(bsz, s_len, SSD_D_INNER) * jax.nn.silu(z.astype(jnp.float32))
    yg = y.reshape(bsz, s_len, G, SSD_D_INNER // G)
    yg = yg * lax.rsqrt(jnp.mean(yg * yg, axis=-1, keepdims=True) + NORM_EPS)
    y = yg.reshape(bsz, s_len, SSD_D_INNER) * norm_g.astype(jnp.float32)
    return y.astype(z.dtype)


def dsa_mixer(q, k, v, iq, ik, iw):
    bsz, s_len = q.shape[:2]
    topk = min(TOPK_MAX, s_len // 4)
    nb = s_len // Q_BLOCK
    idx_scale = (IDX_HEAD_DIM ** -0.5) * (IDX_N_HEADS ** -0.5)
    att_scale = ATT_HEAD_DIM ** -0.5
    key_pos = jnp.arange(s_len, dtype=jnp.int32)

    qb = jnp.moveaxis(q.reshape(bsz, nb, Q_BLOCK, ATT_N_KV, ATT_GROUP, ATT_HEAD_DIM), 1, 0)
    iqb = jnp.moveaxis(iq.reshape(bsz, nb, Q_BLOCK, IDX_N_HEADS, IDX_HEAD_DIM), 1, 0)
    iwb = jnp.moveaxis(iw.reshape(bsz, nb, Q_BLOCK, IDX_N_HEADS), 1, 0)
    tb = key_pos.reshape(nb, Q_BLOCK)

    def block(inp):
        q_i, iq_i, iw_i, t = inp
        dots = jnp.einsum('bqhd,bsd->bqhs', iq_i, ik).astype(jnp.float32)
        score = jnp.einsum('bqhs,bqh->bqs', jax.nn.relu(dots), iw_i.astype(jnp.float32)) * idx_scale
        allowed = key_pos[None, :] <= t[:, None]
        score = jnp.where(allowed[None], score, -jnp.inf)
        _, sel = lax.top_k(score, topk)
        valid = sel <= t[None, :, None]
        kg = jax.vmap(lambda kb, ib: kb[ib])(k, sel)
        vg = jax.vmap(lambda vb, ib: vb[ib])(v, sel)
        logits = jnp.einsum('bqkgd,bqjkd->bqkgj', q_i, kg).astype(jnp.float32) * att_scale
        logits = jnp.where(valid[:, :, None, None, :], logits, -jnp.inf)
        p = jax.nn.softmax(logits, axis=-1)
        o = jnp.einsum('bqkgj,bqjkd->bqkgd', p.astype(v.dtype), vg)
        return o.reshape(bsz, Q_BLOCK, ATT_N_HEADS * ATT_HEAD_DIM)

    out = lax.map(block, (qb, iqb, iwb, tb))
    return jnp.moveaxis(out, 0, 1).reshape(bsz, s_len, ATT_N_HEADS * ATT_HEAD_DIM)


def memory_cross_attention(h, mem_n, w_q, w_kv, w_o):
    bsz, s_len, _ = h.shape
    q = (h @ w_q).reshape(bsz, s_len, MEM_HEADS, MEM_HEAD_DIM)
    kv = mem_n @ w_kv
    km = kv[..., :D_MODEL].reshape(bsz, -1, MEM_HEADS, MEM_HEAD_DIM)
    vm = kv[..., D_MODEL:].reshape(bsz, -1, MEM_HEADS, MEM_HEAD_DIM)
    logits = jnp.einsum('bshd,bmhd->bhsm', q, km).astype(jnp.float32) * (MEM_HEAD_DIM ** -0.5)
    p = jax.nn.softmax(logits, axis=-1).astype(vm.dtype)
    o = jnp.einsum('bhsm,bmhd->bshd', p, vm).reshape(bsz, s_len, D_MODEL)
    return o @ w_o


def swiglu(h, w_in, w_out):
    gu = h @ w_in
    return (jax.nn.silu(gu[..., :D_FF]) * gu[..., D_FF:]) @ w_out


def setup_inputs(seed: int = 0) -> dict:
    key = jax.random.key(seed)
    ks = jax.random.split(key, 32)
    f32 = jnp.float32

    def dense(k, shape):
        return jax.random.normal(k, shape, f32) * (shape[-2] ** -0.5)

    def gain(k, shape):
        return 1.0 + 0.05 * jax.random.normal(k, shape, f32)

    x = jax.random.normal(ks[0], (BATCH, SEQ, D_MODEL), f32)
    mem = jax.random.normal(ks[1], (BATCH, N_MEM, D_MODEL), f32)
    offsets = jax.random.randint(ks[2], (BATCH, 1), 0, 1024, dtype=jnp.int32)
    positions = offsets + jnp.arange(SEQ, dtype=jnp.int32)[None, :]

    dt0 = jnp.exp(jax.random.uniform(ks[3], (DEPTH, SSD_N_HEADS), f32,
                                     minval=math.log(1e-3), maxval=math.log(1e-1)))
    ssd_dt_bias = dt0 + jnp.log(-jnp.expm1(-dt0))
    ssd_a_log = jnp.log(jax.random.uniform(ks[4], (DEPTH, SSD_N_HEADS), f32, minval=1.0, maxval=16.0))

    return {
        "x": x,
        "mem": mem,
        "positions": positions,
        "norm_mix_pre": gain(ks[5], (DEPTH, D_MODEL)),
        "norm_mix_post": gain(ks[6], (DEPTH, D_MODEL)),
        "w_in": dense(ks[7], (DEPTH, D_MODEL, D_IN_PROJ)),
        "ssd_conv_w": 0.5 * jax.random.normal(ks[8], (DEPTH, SSD_CONV, SSD_CONV_DIM), f32),
        "ssd_conv_b": 0.01 * jax.random.normal(ks[9], (DEPTH, SSD_CONV_DIM), f32),
        "ssd_dt_bias": ssd_dt_bias,
        "ssd_a_log": ssd_a_log,
        "ssd_d": 1.0 + 0.1 * jax.random.normal(ks[10], (DEPTH, SSD_N_HEADS), f32),
        "ssd_norm": gain(ks[11], (DEPTH, SSD_D_INNER)),
        "w_br_ssd": dense(ks[12], (DEPTH, SSD_D_INNER, D_MODEL)),
        "w_br_att": dense(ks[13], (DEPTH, ATT_N_HEADS * ATT_HEAD_DIM, D_MODEL)),
        "w_out": dense(ks[14], (DEPTH, D_MODEL, D_MODEL)),
        "norm_mem_pre": gain(ks[15], (DEPTH, D_MODEL)),
        "norm_mem_post": gain(ks[16], (DEPTH, D_MODEL)),
        "mem_norm": gain(ks[17], (DEPTH, D_MODEL)),
        "w_mem_q": dense(ks[18], (DEPTH, D_MODEL, D_MODEL)),
        "w_mem_kv": dense(ks[19], (DEPTH, D_MODEL, 2 * D_MODEL)),
        "w_mem_o": dense(ks[20], (DEPTH, D_MODEL, D_MODEL)),
        "norm_ffn_pre": gain(ks[21], (DEPTH, D_MODEL)),
        "norm_ffn_post": gain(ks[22], (DEPTH, D_MODEL)),
        "w_ffn_in": dense(ks[23], (DEPTH, D_MODEL, 2 * D_FF)),
        "w_ffn_out": dense(ks[24], (DEPTH, D_FF, D_MODEL)),
    }


def reference(x, mem, positions, norm_mix_pre, norm_mix_post, w_in, ssd_conv_w, ssd_conv_b,
              ssd_dt_bias, ssd_a_log, ssd_d, ssd_norm, w_br_ssd, w_br_att, w_out,
              norm_mem_pre, norm_mem_post, mem_norm, w_mem_q, w_mem_kv, w_mem_o,
              norm_ffn_pre, norm_ffn_post, w_ffn_in, w_ffn_out):
    bsz, s_len, _ = x.shape
    split_points = np.cumsum(IN_PROJ_SIZES)[:-1].tolist()
    for l in range(DEPTH):
        h = rmsnorm(x, norm_mix_pre[l])
        proj = h @ w_in[l]
        (z, xbc, dt_raw, q, k, v, iq, ik, iw, g_ssd, g_att) = jnp.split(proj, split_points, axis=-1)

        y_ssd = ssd_mixer(z, xbc, dt_raw, ssd_conv_w[l], ssd_conv_b[l], ssd_dt_bias[l],
                          ssd_a_log[l], ssd_d[l], ssd_norm[l])
        u_ssd = y_ssd @ w_br_ssd[l]

        q = rope_partial(q.reshape(bsz, s_len, ATT_N_HEADS, ATT_HEAD_DIM), positions)
        k = rope_partial(k.reshape(bsz, s_len, ATT_N_KV, ATT_HEAD_DIM), positions)
        v = v.reshape(bsz, s_len, ATT_N_KV, ATT_HEAD_DIM)
        iq = rope_partial(iq.reshape(bsz, s_len, IDX_N_HEADS, IDX_HEAD_DIM), positions)
        ik = rope_partial(ik[:, :, None, :], positions)[:, :, 0, :]
        u_att = dsa_mixer(q, k, v, iq, ik, iw) @ w_br_att[l]

        merged = jax.nn.sigmoid(g_ssd) * u_ssd + jax.nn.sigmoid(g_att) * u_att
        x = x + rmsnorm(merged @ w_out[l], norm_mix_post[l])

        h = rmsnorm(x, norm_mem_pre[l])
        mem_n = rmsnorm(mem, mem_norm[l])
        x = x + rmsnorm(memory_cross_attention(h, mem_n, w_mem_q[l], w_mem_kv[l], w_mem_o[l]),
                        norm_mem_post[l])

        h = rmsnorm(x, norm_ffn_pre[l])
        x = x + rmsnorm(swiglu(h, w_ffn_in[l], w_ffn_out[l]), norm_ffn_post[l])
    return x
```

```python
import functools
import math

import jax
import jax.numpy as jnp
from jax import lax
from jax.experimental import pallas as pl
from jax.experimental.pallas import tpu as pltpu

F32 = jnp.float32
BF16 = jnp.bfloat16
I32 = jnp.int32

D_MODEL = 2048
N_MEM = 256
SSD_D_INNER = 4096
SSD_HEAD_DIM = 64
SSD_N_HEADS = 64
SSD_N_GROUPS = 8
SSD_HPG = SSD_N_HEADS // SSD_N_GROUPS
SSD_GW = SSD_D_INNER // SSD_N_GROUPS
SSD_D_STATE = 128
SSD_CONV = 4
SSD_CHUNK = 128
SSD_XBC_GW = SSD_GW + 2 * SSD_D_STATE
ATT_HEAD_DIM = 128
ATT_N_HEADS = 16
ATT_N_KV = 4
ATT_GROUP = ATT_N_HEADS // ATT_N_KV
IDX_N_HEADS = 16
IDX_HEAD_DIM = 64
TOPK_MAX = 256
ROPE_THETA = 500000.0
ROPE_FRACTION = 4
MEM_HEADS = 4
MEM_HEAD_DIM = D_MODEL // MEM_HEADS
D_FF = 5632
NORM_EPS = 1e-6
IN_PROJ_SIZES = (4096, 6144, 64, 2048, 512, 512, 1024, 64, 16, 2048, 2048)

LANES = 128
SUBLANES = 8
V7X_VMEM_BYTES = 64 * 1024 * 1024
VMEM_LIMIT = V7X_VMEM_BYTES - 8 * 1024 * 1024

INT_MIN = -(2 ** 31)
NEG_BIG = -1e30

_NT = (((1,), (1,)), ((), ()))


def _cparams(sem):
    return pltpu.CompilerParams(dimension_semantics=sem, vmem_limit_bytes=VMEM_LIMIT)


def _rmsnorm_kernel(x_ref, g_ref, o_ref):
    x = x_ref[...]
    ms = jnp.mean(x * x, axis=-1, keepdims=True)
    o_ref[...] = (x * lax.rsqrt(ms + NORM_EPS) * g_ref[...]).astype(o_ref.dtype)


def rmsnorm_bf16(x2, g, tm=256):
    m, d = x2.shape
    tm = min(tm, m)
    return pl.pallas_call(
        _rmsnorm_kernel,
        grid=(m // tm,),
        in_specs=[pl.BlockSpec((tm, d), lambda i: (i, 0)),
                  pl.BlockSpec((1, d), lambda i: (0, 0))],
        out_specs=pl.BlockSpec((tm, d), lambda i: (i, 0)),
        out_shape=jax.ShapeDtypeStruct((m, d), BF16),
        compiler_params=_cparams(("parallel",)),
        name="rmsnorm",
    )(x2, g.reshape(1, d))


def _epi_none(acc):
    return acc


def _epi_sigmoid(acc):
    return jax.nn.sigmoid(acc)


def _epi_gate_mul(acc, gate_ref):
    return acc * gate_ref[...].astype(F32)


def _epi_gate_mul_add(acc, gate_ref, add_ref):
    return acc * gate_ref[...].astype(F32) + add_ref[...].astype(F32)


def _epi_rope(shift, n_rot, acc, c_ref, s1_ref, s2_ref):
    c = c_ref[...]
    s1 = s1_ref[...]
    s2 = s2_ref[...]
    outs = []
    for j in range(acc.shape[1] // LANES):
        sl = acc[:, j * LANES:(j + 1) * LANES]
        if j < n_rot:
            sl = (sl * c + pltpu.roll(sl, shift, 1) * s1
                  + pltpu.roll(sl, LANES - shift, 1) * s2)
        outs.append(sl)
    return outs[0] if len(outs) == 1 else jnp.concatenate(outs, axis=1)


def _epi_norm_res(acc, g_ref, x_ref):
    ms = jnp.mean(acc * acc, axis=-1, keepdims=True)
    return x_ref[...] + acc * lax.rsqrt(ms + NORM_EPS) * g_ref[...]


def _mm_kernel(epi, n_extra, nk, a_ref, w_ref, *refs):
    extras = refs[:n_extra]
    o_ref = refs[n_extra]
    a = a_ref[...]
    a = a.reshape(a.shape[-2], a.shape[-1])
    part = jnp.dot(a, w_ref[...], preferred_element_type=F32)
    if nk == 1:
        o_ref[...] = epi(part, *extras).astype(o_ref.dtype).reshape(o_ref.shape)
        return
    acc_ref = refs[n_extra + 1]
    k = pl.program_id(2)

    @pl.when(k == 0)
    def _():
        acc_ref[...] = part

    @pl.when(k > 0)
    def _():
        acc_ref[...] += part

    @pl.when(k == nk - 1)
    def _():
        o_ref[...] = epi(acc_ref[...], *extras).astype(o_ref.dtype).reshape(o_ref.shape)


def matmul(a, w, *, tm, tn, tk=None, out_dtype=F32, epi=_epi_none, extras=(),
           a_gm=False, out_gm=False, name="matmul"):
    kdim, n = w.shape
    if a_gm:
        nk, m, tk = a.shape
        assert nk * tk == kdim
    else:
        m = a.shape[0]
        tk = kdim if tk is None else tk
        assert kdim % tk == 0
        nk = kdim // tk
    tm = min(tm, m)
    assert m % tm == 0 and n % tn == 0
    if a_gm:
        a_spec = pl.BlockSpec((1, tm, tk), lambda i, j, k: (k, i, 0))
    else:
        a_spec = pl.BlockSpec((tm, tk), lambda i, j, k: (i, k))
    in_specs = [a_spec, pl.BlockSpec((tk, tn), lambda i, j, k: (k, j))]
    args = [a, w]
    for ex in extras:
        arr, kind = ex[0], ex[1]
        off = ex[2] if len(ex) > 2 else 0
        if kind == "mn":
            in_specs.append(pl.BlockSpec((tm, tn), lambda i, j, k, off=off: (i, j + off)))
        elif kind == "m128":
            in_specs.append(pl.BlockSpec((tm, LANES), lambda i, j, k: (i, 0)))
        elif kind == "n":
            in_specs.append(pl.BlockSpec((1, tn), lambda i, j, k: (0, j)))
        else:
            raise ValueError(kind)
        args.append(arr)
    if out_gm:
        out_spec = pl.BlockSpec((1, tm, tn), lambda i, j, k: (j, i, 0))
        out_shape = jax.ShapeDtypeStruct((n // tn, m, tn), out_dtype)
    else:
        out_spec = pl.BlockSpec((tm, tn), lambda i, j, k: (i, j))
        out_shape = jax.ShapeDtypeStruct((m, n), out_dtype)
    scratch = [pltpu.VMEM((tm, tn), F32)] if nk > 1 else []
    return pl.pallas_call(
        functools.partial(_mm_kernel, epi, len(extras), nk),
        grid=(m // tm, n // tn, nk),
        in_specs=in_specs,
        out_specs=out_spec,
        out_shape=out_shape,
        scratch_shapes=scratch,
        compiler_params=_cparams(("parallel", "parallel", "arbitrary")),
        name=name,
    )(*args)


def _swiglu_kernel(a_ref, wg_ref, wu_ref, o_ref):
    a = a_ref[...]
    g = jnp.dot(a, wg_ref[...], preferred_element_type=F32)
    u = jnp.dot(a, wu_ref[...], preferred_element_type=F32)
    o_ref[...] = (g * jax.nn.sigmoid(g) * u).astype(o_ref.dtype)


def swiglu_in(a, wg, wu, *, tm, tn):
    m, kdim = a.shape
    n = wg.shape[1]
    tm = min(tm, m)
    return pl.pallas_call(
        _swiglu_kernel,
        grid=(m // tm, n // tn),
        in_specs=[pl.BlockSpec((tm, kdim), lambda i, j: (i, 0)),
                  pl.BlockSpec((kdim, tn), lambda i, j: (0, j)),
                  pl.BlockSpec((kdim, tn), lambda i, j: (0, j))],
        out_specs=pl.BlockSpec((tm, tn), lambda i, j: (i, j)),
        out_shape=jax.ShapeDtypeStruct((m, n), BF16),
        compiler_params=_cparams(("parallel", "parallel")),
        name="swiglu_in",
    )(a, wg, wu)


def _split3_bf16(x):
    h1 = x.astype(BF16)
    r1 = x - h1.astype(F32)
    h2 = r1.astype(BF16)
    h3 = (r1 - h2.astype(F32)).astype(BF16)
    return h1, h2, h3


def _ssd_kernel(z_ref, xbc_ref, dt_ref, cw_ref, cb_ref, dtb_ref, alog_ref, d_ref, ng_ref,
                y_ref, ubuf, st_scr):
    q = SSD_CHUNK
    c_idx = pl.program_id(1)

    @pl.when(c_idx == 0)
    def _():
        ubuf[:, 0:SUBLANES, :] = jnp.zeros((SSD_N_GROUPS, SUBLANES, SSD_XBC_GW), F32)
        st_scr[...] = jnp.zeros(st_scr.shape, F32)

    row = lax.broadcasted_iota(I32, (q, q), 0)
    col = lax.broadcasted_iota(I32, (q, q), 1)
    causal = row >= col
    tril16 = causal.astype(F32).astype(BF16)
    lo = lax.broadcasted_iota(I32, (q, LANES), 1) < SSD_HEAD_DIM

    def pairx(v, j0):
        r = v.shape[0]
        return jnp.where(lo[:r], v[:, j0:j0 + 1], v[:, j0 + 1:j0 + 2])

    def group(g, carry):
        ubuf[g, SUBLANES:SUBLANES + q, :] = xbc_ref[g]
        w = cw_ref[g]
        conv = cb_ref[g]
        for k in range(SSD_CONV):
            conv = conv + ubuf[g, pl.ds(SUBLANES - (SSD_CONV - 1) + k, q), :] * w[k:k + 1, :]
        ubuf[g, 0:SUBLANES, :] = ubuf[g, q:q + SUBLANES, :]
        v = conv * jax.nn.sigmoid(conv)
        xs = v[:, :SSD_GW]
        bm = v[:, SSD_GW:SSD_GW + SSD_D_STATE]
        cm = v[:, SSD_GW + SSD_D_STATE:]

        dtr = dt_ref[g] + dtb_ref[g]
        dt = jnp.maximum(dtr, 0.0) + jnp.log1p(jnp.exp(-jnp.abs(dtr)))
        da = dt * (-jnp.exp(alog_ref[g]))
        h1, h2, h3 = _split3_bf16(da)
        cs = (jnp.dot(tril16, h1, preferred_element_type=F32)
              + jnp.dot(tril16, h2, preferred_element_type=F32)
              + jnp.dot(tril16, h3, preferred_element_type=F32))
        cs_t = cs.T
        last = cs[q - 1:q, :]
        ecs = jnp.exp(cs)
        wend = jnp.exp(last - cs)
        elast = jnp.exp(last)

        bm16 = bm.astype(BF16)
        cm16 = cm.astype(BF16)
        cb = lax.dot_general(cm16, bm16, _NT, preferred_element_type=F32)
        st_t = st_scr[g]
        ystate = jnp.dot(cm16, st_t.astype(BF16), preferred_element_type=F32)
        dsk = d_ref[g]

        ys, xws, els = [], [], []
        for p in range(SSD_HPG // 2):
            j0 = 2 * p
            xs_p = xs[:, p * LANES:(p + 1) * LANES]
            xdt = xs_p * pairx(dt, j0)
            lmats = []
            for j in (j0, j0 + 1):
                diff = cs[:, j:j + 1] - cs_t[j:j + 1, :]
                dec = jnp.exp(jnp.where(causal, diff, -jnp.inf))
                lmats.append((cb * dec).astype(BF16))
            lmat = jnp.concatenate(lmats, axis=1)
            rhs = jnp.concatenate([jnp.where(lo, xdt, 0.0), jnp.where(lo, 0.0, xdt)],
                                  axis=0).astype(BF16)
            y = jnp.dot(lmat, rhs, preferred_element_type=F32)
            y = y + ystate[:, p * LANES:(p + 1) * LANES] * pairx(ecs, j0)
            y = y + pairx(dsk, j0) * xs_p
            ys.append(y)
            xws.append((xdt * pairx(wend, j0)).astype(BF16))
            els.append(pairx(elast, j0))
        y = jnp.concatenate(ys, axis=1)
        xw = jnp.concatenate(xws, axis=1)
        el = jnp.concatenate(els, axis=1)
        st_scr[g] = st_t * el + jnp.dot(bm.T.astype(BF16), xw, preferred_element_type=F32)

        zz = z_ref[g]
        y = y * (zz * jax.nn.sigmoid(zz))
        y = y * lax.rsqrt(jnp.mean(y * y, axis=-1, keepdims=True) + NORM_EPS) * ng_ref[g]
        y_ref[g] = y.astype(y_ref.dtype)
        return carry

    lax.fori_loop(0, SSD_N_GROUPS, group, 0)


def ssd_mixer(z_g, xbc_g, dt_g, cw_g, cb_g, dtb_g, alog_g, d_g, ng_g, *, bsz, s_len):
    g, q = SSD_N_GROUPS, SSD_CHUNK
    m = bsz * s_len
    nc = s_len // q

    def act(w):
        return pl.BlockSpec((g, q, w), lambda b, c: (0, b * nc + c, 0))

    def par(r, w):
        return pl.BlockSpec((g, r, w), lambda b, c: (0, 0, 0))

    return pl.pallas_call(
        _ssd_kernel,
        grid=(bsz, nc),
        in_specs=[act(SSD_GW), act(SSD_XBC_GW), act(LANES),
                  par(SSD_CONV, SSD_XBC_GW), par(1, SSD_XBC_GW),
                  par(1, LANES), par(1, LANES), par(1, LANES), par(1, SSD_GW)],
        out_specs=act(SSD_GW),
        out_shape=jax.ShapeDtypeStruct((g, m, SSD_GW), BF16),
        scratch_shapes=[pltpu.VMEM((g, q + SUBLANES, SSD_XBC_GW), F32),
                        pltpu.VMEM((g, SSD_D_STATE, SSD_GW), F32)],
        compiler_params=_cparams(("arbitrary", "arbitrary")),
        name="ssd_mixer",
    )(z_g, xbc_g, dt_g, cw_g, cb_g, dtb_g, alog_g, d_g, ng_g)


def _dsa_kernel(q_ref, iq_ref, smq_ref, k_ref, v_ref, smk_ref, o_ref,
                key_scr, m_scr, l_scr, acc_scr, *, tq, ck, topk):
    i = pl.program_id(1)
    q0 = i * tq
    nch = lax.div(q0 + tq + ck - 1, ck)
    idx_scale = (IDX_HEAD_DIM ** -0.5) * (IDX_N_HEADS ** -0.5)
    att_scale = ATT_HEAD_DIM ** -0.5

    iw = smq_ref[0][:, IDX_HEAD_DIM:IDX_HEAD_DIM + IDX_N_HEADS]
    tpos = q0 + lax.broadcasted_iota(I32, (tq, ck), 0)
    lane_pos = lax.broadcasted_iota(I32, (tq, ck), 1)

    def score_chunk(c, carry):
        s0 = pl.multiple_of(c * ck, ck)
        ikc = smk_ref[0, pl.ds(s0, ck), :][:, :IDX_HEAD_DIM].astype(BF16)
        acc = jnp.zeros((tq, ck), F32)
        for h in range(IDX_N_HEADS):
            iqh = iq_ref[0, :, h * IDX_HEAD_DIM:(h + 1) * IDX_HEAD_DIM]
            d = lax.dot_general(iqh, ikc, _NT, preferred_element_type=F32)
            acc = acc + jnp.maximum(d, 0.0) * iw[:, h:h + 1]
        acc = acc * idx_scale
        bits = pltpu.bitcast(acc, I32)
        key = jnp.where(bits < 0, bits ^ jnp.int32(0x7FFFFFFF), bits)
        key_scr[c] = jnp.where(s0 + lane_pos <= tpos, key, jnp.int32(INT_MIN))
        return carry

    lax.fori_loop(0, nch, score_chunk, 0)

    def count_ge(cand):
        cand_b = jnp.broadcast_to(cand, (tq, LANES))

        def body(c, cnt):
            kc = key_scr[c]
            for u in range(ck // LANES):
                cnt = cnt + (kc[:, u * LANES:(u + 1) * LANES] >= cand_b).astype(I32)
            return cnt

        cnt = lax.fori_loop(0, nch, body, jnp.zeros((tq, LANES), I32))
        return jnp.sum(cnt.astype(F32), axis=1, keepdims=True)

    def bit_step(it, thr):
        cand = thr + lax.shift_left(jnp.int32(1), 31 - it)
        return jnp.where(count_ge(cand) >= float(topk), cand, thr)

    thr = lax.fori_loop(0, 32, bit_step, jnp.full((tq, 1), INT_MIN, I32))
    thr = jnp.maximum(thr, jnp.int32(INT_MIN + 1))

    m_scr[...] = jnp.full(m_scr.shape, NEG_BIG, F32)
    l_scr[...] = jnp.zeros(l_scr.shape, F32)
    acc_scr[...] = jnp.zeros(acc_scr.shape, F32)

    def att_chunk(c, carry):
        s0 = pl.multiple_of(c * ck, ck)
        sel = key_scr[c] >= thr
        for g in range(ATT_N_KV):
            kc = k_ref[0, pl.ds(s0, ck), g * ATT_HEAD_DIM:(g + 1) * ATT_HEAD_DIM]
            vc = v_ref[0, pl.ds(s0, ck), g * ATT_HEAD_DIM:(g + 1) * ATT_HEAD_DIM]
            for hh in range(ATT_GROUP):
                h = g * ATT_GROUP + hh
                qh = q_ref[0, :, h * ATT_HEAD_DIM:(h + 1) * ATT_HEAD_DIM]
                s = lax.dot_general(qh, kc, _NT, preferred_element_type=F32) * att_scale
                s = jnp.where(sel, s, NEG_BIG)
                m_old = m_scr[h]
                m_new = jnp.maximum(m_old, jnp.max(s, axis=1, keepdims=True))
                p = jnp.exp(s - m_new)
                alpha = jnp.exp(m_old - m_new)
                l_scr[h] = alpha * l_scr[h] + jnp.sum(p, axis=1, keepdims=True)
                acc_scr[h] = alpha * acc_scr[h] + jnp.dot(
                    p.astype(BF16), vc, preferred_element_type=F32)
                m_scr[h] = m_new
        return carry

    lax.fori_loop(0, nch, att_chunk, 0)
    for h in range(ATT_N_HEADS):
        o_ref[0, :, h * ATT_HEAD_DIM:(h + 1) * ATT_HEAD_DIM] = (
            acc_scr[h] / l_scr[h]).astype(o_ref.dtype)


def dsa_mixer(q, k, v, iq, small, *, bsz, s_len, tq=128, ck=512):
    topk = min(TOPK_MAX, s_len // 4)
    ck = min(ck, s_len)
    assert ck >= topk and s_len % ck == 0 and s_len % tq == 0
    nq = s_len // tq
    hq = ATT_N_HEADS * ATT_HEAD_DIM
    hk = ATT_N_KV * ATT_HEAD_DIM
    hi = IDX_N_HEADS * IDX_HEAD_DIM
    r3 = lambda a: a.reshape(bsz, s_len, a.shape[-1])
    out = pl.pallas_call(
        functools.partial(_dsa_kernel, tq=tq, ck=ck, topk=topk),
        grid=(bsz, nq),
        in_specs=[pl.BlockSpec((1, tq, hq), lambda b, i: (b, i, 0)),
                  pl.BlockSpec((1, tq, hi), lambda b, i: (b, i, 0)),
                  pl.BlockSpec((1, tq, LANES), lambda b, i: (b, i, 0)),
                  pl.BlockSpec((1, s_len, hk), lambda b, i: (b, 0, 0)),
                  pl.BlockSpec((1, s_len, hk), lambda b, i: (b, 0, 0)),
                  pl.BlockSpec((1, s_len, LANES), lambda b, i: (b, 0, 0))],
        out_specs=pl.BlockSpec((1, tq, hq), lambda b, i: (b, i, 0)),
        out_shape=jax.ShapeDtypeStruct((bsz, s_len, hq), BF16),
        scratch_shapes=[pltpu.VMEM((s_len // ck, tq, ck), I32),
                        pltpu.VMEM((ATT_N_HEADS, tq, 1), F32),
                        pltpu.VMEM((ATT_N_HEADS, tq, 1), F32),
                        pltpu.VMEM((ATT_N_HEADS, tq, ATT_HEAD_DIM), F32)],
        compiler_params=_cparams(("parallel", "arbitrary")),
        name="dsa_mixer",
    )(r3(q), r3(iq), r3(small), r3(k), r3(v), r3(small))
    return out.reshape(bsz * s_len, hq)


def _memattn_kernel(q_ref, kv_ref, o_ref):
    scale = MEM_HEAD_DIM ** -0.5
    for h in range(MEM_HEADS):
        sl = slice(h * MEM_HEAD_DIM, (h + 1) * MEM_HEAD_DIM)
        qh = q_ref[:, sl]
        kh = kv_ref[:, sl]
        vh = kv_ref[:, D_MODEL + h * MEM_HEAD_DIM:D_MODEL + (h + 1) * MEM_HEAD_DIM]
        s = lax.dot_general(qh, kh, _NT, preferred_element_type=F32) * scale
        p = jnp.exp(s - jnp.max(s, axis=1, keepdims=True))
        o = jnp.dot(p.astype(BF16), vh, preferred_element_type=F32)
        o_ref[:, sl] = (o / jnp.sum(p, axis=1, keepdims=True)).astype(o_ref.dtype)


def mem_attention(qm, kv, *, bsz, s_len, tq=512):
    m = bsz * s_len
    tq = min(tq, s_len)
    nq = s_len // tq
    n_mem = kv.shape[0] // bsz
    return pl.pallas_call(
        _memattn_kernel,
        grid=(bsz, nq),
        in_specs=[pl.BlockSpec((tq, D_MODEL), lambda b, i: (b * nq + i, 0)),
                  pl.BlockSpec((n_mem, 2 * D_MODEL), lambda b, i: (b, 0))],
        out_specs=pl.BlockSpec((tq, D_MODEL), lambda b, i: (b * nq + i, 0)),
        out_shape=jax.ShapeDtypeStruct((m, D_MODEL), BF16),
        compiler_params=_cparams(("parallel", "parallel")),
        name="mem_attention",
    )(qm, kv)


def _rope_tables(positions):
    m = positions.size
    pos = positions.astype(F32).reshape(m, 1)

    def unit(head_dim):
        rot = head_dim // ROPE_FRACTION
        half = rot // 2
        inv_freq = jnp.power(ROPE_THETA, -jnp.arange(half, dtype=F32) * 2.0 / rot)
        ang = pos * inv_freq
        cos, sin = jnp.cos(ang), jnp.sin(ang)
        one = jnp.ones((m, head_dim - rot), F32)
        zero = jnp.zeros((m, head_dim - rot), F32)
        zh = jnp.zeros((m, half), F32)
        c = jnp.concatenate([cos, cos, one], axis=1)
        s1 = jnp.concatenate([zh, sin, zero], axis=1)
        s2 = jnp.concatenate([-sin, zh, zero], axis=1)
        return c, s1, s2

    att = unit(ATT_HEAD_DIM)
    ic, is1, is2 = unit(IDX_HEAD_DIM)
    idx = tuple(jnp.concatenate([t, t], axis=1) for t in (ic, is1, is2))
    pad1 = jnp.ones((m, LANES - IDX_HEAD_DIM), F32)
    pad0 = jnp.zeros((m, LANES - IDX_HEAD_DIM), F32)
    small = (jnp.concatenate([ic, pad1], axis=1),
             jnp.concatenate([is1, pad0], axis=1),
             jnp.concatenate([is2, pad0], axis=1))
    return att, idx, small


def _prep_in_proj(w):
    offs = [0]
    for s in IN_PROJ_SIZES:
        offs.append(offs[-1] + s)
    seg = lambda i: w[:, offs[i]:offs[i + 1]]
    z, xbc, dt, q, k, v, iq, ik, iw, g_ssd, g_att = (seg(i) for i in range(11))
    d = w.shape[0]
    g = SSD_N_GROUPS
    xs_w = xbc[:, :SSD_D_INNER].reshape(d, g, SSD_GW)
    b_w = xbc[:, SSD_D_INNER:SSD_D_INNER + g * SSD_D_STATE].reshape(d, g, SSD_D_STATE)
    c_w = xbc[:, SSD_D_INNER + g * SSD_D_STATE:].reshape(d, g, SSD_D_STATE)
    xbc_gm = jnp.concatenate([xs_w, b_w, c_w], axis=2).reshape(d, g * SSD_XBC_GW)
    dt_pad = jnp.pad(dt.reshape(d, g, SSD_HPG), ((0, 0), (0, 0), (0, LANES - SSD_HPG)))
    small = jnp.concatenate(
        [ik, iw, jnp.zeros((d, LANES - IDX_HEAD_DIM - IDX_N_HEADS), w.dtype)], axis=1)
    c16 = lambda a: a.astype(BF16)
    return dict(z=c16(z), xbc=c16(xbc_gm), dt=c16(dt_pad.reshape(d, g * LANES)),
                q=c16(q), k=c16(k), v=c16(v), iq=c16(iq), small=c16(small),
                gates=c16(jnp.concatenate([g_ssd, g_att], axis=1)))


def _prep_ssd_params(conv_w, conv_b, dt_bias, a_log, d_skip, norm_g):
    g = SSD_N_GROUPS

    def gm_channels(a):
        r = a.shape[0]
        xs = a[:, :SSD_D_INNER].reshape(r, g, SSD_GW)
        b = a[:, SSD_D_INNER:SSD_D_INNER + g * SSD_D_STATE].reshape(r, g, SSD_D_STATE)
        c = a[:, SSD_D_INNER + g * SSD_D_STATE:].reshape(r, g, SSD_D_STATE)
        return jnp.transpose(jnp.concatenate([xs, b, c], axis=2), (1, 0, 2))

    def gm_heads(a):
        return jnp.pad(a.reshape(g, 1, SSD_HPG), ((0, 0), (0, 0), (0, LANES - SSD_HPG)))

    return (gm_channels(conv_w), gm_channels(conv_b[None, :]), gm_heads(dt_bias),
            gm_heads(a_log), gm_heads(d_skip), norm_g.reshape(g, 1, SSD_GW))


def kernel(x, mem, positions, norm_mix_pre, norm_mix_post, w_in, ssd_conv_w, ssd_conv_b,
           ssd_dt_bias, ssd_a_log, ssd_d, ssd_norm, w_br_ssd, w_br_att, w_out,
           norm_mem_pre, norm_mem_post, mem_norm, w_mem_q, w_mem_kv, w_mem_o,
           norm_ffn_pre, norm_ffn_post, w_ffn_in, w_ffn_out):
    bsz, s_len, d = x.shape
    m = bsz * s_len
    depth = w_in.shape[0]
    x2 = x.reshape(m, d)
    mem2 = mem.reshape(bsz * mem.shape[1], d)
    rope_att, rope_idx, rope_small = _rope_tables(positions)
    rope_ex = lambda t: tuple((a, "m128") for a in t)
    tm = 1024
    gcol = D_MODEL // 1024

    for l in range(depth):
        wp = _prep_in_proj(w_in[l])
        hn = rmsnorm_bf16(x2, norm_mix_pre[l])
        z_g = matmul(hn, wp["z"], tm=tm, tn=SSD_GW, out_gm=True, name="proj_z")
        xbc_g = matmul(hn, wp["xbc"], tm=tm, tn=SSD_XBC_GW, out_gm=True, name="proj_xbc")
        dt_g = matmul(hn, wp["dt"], tm=tm, tn=LANES, out_gm=True, name="proj_dt")
        q = matmul(hn, wp["q"], tm=tm, tn=1024, out_dtype=BF16, name="proj_q",
                   epi=functools.partial(_epi_rope, ATT_HEAD_DIM // ROPE_FRACTION // 2, 1024 // LANES),
                   extras=rope_ex(rope_att))
        k = matmul(hn, wp["k"], tm=tm, tn=512, out_dtype=BF16, name="proj_k",
                   epi=functools.partial(_epi_rope, ATT_HEAD_DIM // ROPE_FRACTION // 2, 512 // LANES),
                   extras=rope_ex(rope_att))
        v = matmul(hn, wp["v"], tm=tm, tn=512, out_dtype=BF16, name="proj_v")
        iq = matmul(hn, wp["iq"], tm=tm, tn=1024, out_dtype=BF16, name="proj_iq",
                    epi=functools.partial(_epi_rope, IDX_HEAD_DIM // ROPE_FRACTION // 2, 1024 // LANES),
                    extras=rope_ex(rope_idx))
        small = matmul(hn, wp["small"], tm=tm, tn=LANES, name="proj_small",
                       epi=functools.partial(_epi_rope, IDX_HEAD_DIM // ROPE_FRACTION // 2, 1),
                       extras=rope_ex(rope_small))
        gates = matmul(hn, wp["gates"], tm=tm, tn=1024, epi=_epi_sigmoid, name="proj_gates")

        ssd_par = _prep_ssd_params(ssd_conv_w[l], ssd_conv_b[l], ssd_dt_bias[l],
                                   ssd_a_log[l], ssd_d[l], ssd_norm[l])
        y_g = ssd_mixer(z_g, xbc_g, dt_g, *ssd_par, bsz=bsz, s_len=s_len)
        att = dsa_mixer(q, k, v, iq, small, bsz=bsz, s_len=s_len)

        u_ssd = matmul(y_g, w_br_ssd[l].astype(BF16), tm=tm, tn=1024, a_gm=True,
                       epi=_epi_gate_mul, extras=((gates, "mn", 0),), name="branch_ssd")
        merged = matmul(att, w_br_att[l].astype(BF16), tm=tm, tn=1024, out_dtype=BF16,
                        epi=_epi_gate_mul_add, extras=((gates, "mn", gcol), (u_ssd, "mn", 0)),
                        name="branch_att_merge")
        x2 = matmul(merged, w_out[l].astype(BF16), tm=256, tn=D_MODEL, epi=_epi_norm_res,
                    extras=((norm_mix_post[l].reshape(1, d), "n"), (x2, "mn")), name="mix_out")

        hn = rmsnorm_bf16(x2, norm_mem_pre[l])
        qm = matmul(hn, w_mem_q[l].astype(BF16), tm=tm, tn=1024, out_dtype=BF16, name="mem_q")
        mem_n = rmsnorm_bf16(mem2, mem_norm[l])
        kvm = matmul(mem_n, w_mem_kv[l].astype(BF16), tm=tm, tn=1024, out_dtype=BF16, name="mem_kv")
        om = mem_attention(qm, kvm, bsz=bsz, s_len=s_len)
        x2 = matmul(om, w_mem_o[l].astype(BF16), tm=256, tn=D_MODEL, epi=_epi_norm_res,
                    extras=((norm_mem_post[l].reshape(1, d), "n"), (x2, "mn")), name="mem_out")

        hn = rmsnorm_bf16(x2, norm_ffn_pre[l])
        wf = w_ffn_in[l]
        hf = swiglu_in(hn, wf[:, :D_FF].astype(BF16), wf[:, D_FF:].astype(BF16), tm=tm, tn=512)
        x2 = matmul(hf, w_ffn_out[l].astype(BF16), tm=512, tn=D_MODEL, tk=D_FF // 4,
                    epi=_epi_norm_res,
                    extras=((norm_ffn_post[l].reshape(1, d), "n"), (x2, "mn")), name="ffn_out")

    return x2.reshape(bsz, s_len, d)
```

```python
import functools
import math

import jax
import jax.numpy as jnp
from jax import lax
from jax.experimental import pallas as pl
from jax.experimental.pallas import tpu as pltpu

F32 = jnp.float32
BF16 = jnp.bfloat16
I32 = jnp.int32

D_MODEL = 2048
N_MEM = 256
SSD_D_INNER = 4096
SSD_HEAD_DIM = 64
SSD_N_HEADS = 64
SSD_N_GROUPS = 8
SSD_HPG = SSD_N_HEADS // SSD_N_GROUPS
SSD_GW = SSD_D_INNER // SSD_N_GROUPS
SSD_D_STATE = 128
SSD_CONV = 4
SSD_CHUNK = 128
SSD_XBC_GW = SSD_GW + 2 * SSD_D_STATE
ATT_HEAD_DIM = 128
ATT_N_HEADS = 16
ATT_N_KV = 4
ATT_GROUP = ATT_N_HEADS // ATT_N_KV
IDX_N_HEADS = 16
IDX_HEAD_DIM = 64
TOPK_MAX = 256
ROPE_THETA = 500000.0
ROPE_FRACTION = 4
MEM_HEADS = 4
MEM_HEAD_DIM = D_MODEL // MEM_HEADS
D_FF = 5632
NORM_EPS = 1e-6
IN_PROJ_SIZES = (4096, 6144, 64, 2048, 512, 512, 1024, 64, 16, 2048, 2048)

LANES = 128
SUBLANES = 8
V7X_VMEM_BYTES = 64 * 1024 * 1024
VMEM_LIMIT = V7X_VMEM_BYTES - 8 * 1024 * 1024

INT_MIN = -(2 ** 31)
NEG_BIG = -1e30

_NT = (((1,), (1,)), ((), ()))


def _cparams(sem):
    return pltpu.CompilerParams(dimension_semantics=sem, vmem_limit_bytes=VMEM_LIMIT)


def _rmsnorm_kernel(x_ref, g_ref, o_ref):
    x = x_ref[...]
    ms = jnp.mean(x * x, axis=-1, keepdims=True)
    o_ref[...] = (x * lax.rsqrt(ms + NORM_EPS) * g_ref[...]).astype(o_ref.dtype)


def rmsnorm_bf16(x2, g, tm=256):
    m, d = x2.shape
    tm = min(tm, m)
    return pl.pallas_call(
        _rmsnorm_kernel,
        grid=(m // tm,),
        in_specs=[pl.BlockSpec((tm, d), lambda i: (i, 0)),
                  pl.BlockSpec((1, d), lambda i: (0, 0))],
        out_specs=pl.BlockSpec((tm, d), lambda i: (i, 0)),
        out_shape=jax.ShapeDtypeStruct((m, d), BF16),
        compiler_params=_cparams(("parallel",)),
        name="rmsnorm",
    )(x2, g.reshape(1, d))


def _epi_none(acc):
    return acc


def _epi_sigmoid(acc):
    return jax.nn.sigmoid(acc)


def _epi_gate_mul(acc, gate_ref):
    return acc * gate_ref[...].astype(F32)


def _epi_gate_mul_add(acc, gate_ref, add_ref):
    return acc * gate_ref[...].astype(F32) + add_ref[...].astype(F32)


def _epi_rope(shift, n_rot, acc, c_ref, s1_ref, s2_ref):
    c = c_ref[...]
    s1 = s1_ref[...]
    s2 = s2_ref[...]
    outs = []
    for j in range(acc.shape[1] // LANES):
        sl = acc[:, j * LANES:(j + 1) * LANES]
        if j < n_rot:
            sl = (sl * c + pltpu.roll(sl, shift, 1) * s1
                  + pltpu.roll(sl, LANES - shift, 1) * s2)
        outs.append(sl)
    return outs[0] if len(outs) == 1 else jnp.concatenate(outs, axis=1)


def _epi_norm_res(acc, g_ref, x_ref):
    ms = jnp.mean(acc * acc, axis=-1, keepdims=True)
    return x_ref[...] + acc * lax.rsqrt(ms + NORM_EPS) * g_ref[...]


def _mm_kernel(epi, n_extra, nk, a_ref, w_ref, *refs):
    extras = refs[:n_extra]
    o_ref = refs[n_extra]
    a = a_ref[...]
    a = a.reshape(a.shape[-2], a.shape[-1])
    part = jnp.dot(a, w_ref[...], preferred_element_type=F32)
    if nk == 1:
        o_ref[...] = epi(part, *extras).astype(o_ref.dtype).reshape(o_ref.shape)
        return
    acc_ref = refs[n_extra + 1]
    k = pl.program_id(2)

    @pl.when(k == 0)
    def _():
        acc_ref[...] = part

    @pl.when(k > 0)
    def _():
        acc_ref[...] += part

    @pl.when(k == nk - 1)
    def _():
        o_ref[...] = epi(acc_ref[...], *extras).astype(o_ref.dtype).reshape(o_ref.shape)


def matmul(a, w, *, tm, tn, tk=None, out_dtype=F32, epi=_epi_none, extras=(),
           a_gm=False, out_gm=False, name="matmul"):
    kdim, n = w.shape
    if a_gm:
        nk, m, tk = a.shape
        assert nk * tk == kdim
    else:
        m = a.shape[0]
        tk = kdim if tk is None else tk
        assert kdim % tk == 0
        nk = kdim // tk
    tm = min(tm, m)
    assert m % tm == 0 and n % tn == 0
    if a_gm:
        a_spec = pl.BlockSpec((1, tm, tk), lambda i, j, k: (k, i, 0))
    else:
        a_spec = pl.BlockSpec((tm, tk), lambda i, j, k: (i, k))
    in_specs = [a_spec, pl.BlockSpec((tk, tn), lambda i, j, k: (k, j))]
    args = [a, w]
    for ex in extras:
        arr, kind = ex[0], ex[1]
        off = ex[2] if len(ex) > 2 else 0
        if kind == "mn":
            in_specs.append(pl.BlockSpec((tm, tn), lambda i, j, k, off=off: (i, j + off)))
        elif kind == "m128":
            in_specs.append(pl.BlockSpec((tm, LANES), lambda i, j, k: (i, 0)))
        elif kind == "n":
            in_specs.append(pl.BlockSpec((1, tn), lambda i, j, k: (0, j)))
        else:
            raise ValueError(kind)
        args.append(arr)
    if out_gm:
        out_spec = pl.BlockSpec((1, tm, tn), lambda i, j, k: (j, i, 0))
        out_shape = jax.ShapeDtypeStruct((n // tn, m, tn), out_dtype)
    else:
        out_spec = pl.BlockSpec((tm, tn), lambda i, j, k: (i, j))
        out_shape = jax.ShapeDtypeStruct((m, n), out_dtype)
    scratch = [pltpu.VMEM((tm, tn), F32)] if nk > 1 else []
    return pl.pallas_call(
        functools.partial(_mm_kernel, epi, len(extras), nk),
        grid=(m // tm, n // tn, nk),
        in_specs=in_specs,
        out_specs=out_spec,
        out_shape=out_shape,
        scratch_shapes=scratch,
        compiler_params=_cparams(("parallel", "parallel", "arbitrary")),
        name=name,
    )(*args)


def _swiglu_kernel(a_ref, wg_ref, wu_ref, o_ref):
    a = a_ref[...]
    g = jnp.dot(a, wg_ref[...], preferred_element_type=F32)
    u = jnp.dot(a, wu_ref[...], preferred_element_type=F32)
    o_ref[...] = (g * jax.nn.sigmoid(g) * u).astype(o_ref.dtype)


def swiglu_in(a, wg, wu, *, tm, tn):
    m, kdim = a.shape
    n = wg.shape[1]
    tm = min(tm, m)
    return pl.pallas_call(
        _swiglu_kernel,
        grid=(m // tm, n // tn),
        in_specs=[pl.BlockSpec((tm, kdim), lambda i, j: (i, 0)),
                  pl.BlockSpec((kdim, tn), lambda i, j: (0, j)),
                  pl.BlockSpec((kdim, tn), lambda i, j: (0, j))],
        out_specs=pl.BlockSpec((tm, tn), lambda i, j: (i, j)),
        out_shape=jax.ShapeDtypeStruct((m, n), BF16),
        compiler_params=_cparams(("parallel", "parallel")),
        name="swiglu_in",
    )(a, wg, wu)


def _split3_bf16(x):
    h1 = x.astype(BF16)
    r1 = x - h1.astype(F32)
    h2 = r1.astype(BF16)
    h3 = (r1 - h2.astype(F32)).astype(BF16)
    return h1, h2, h3


def _ssd_kernel(z_ref, xbc_ref, dt_ref, cw_ref, cb_ref, dtb_ref, alog_ref, d_ref, ng_ref,
                y_ref, ubuf, st_scr):
    q = SSD_CHUNK
    c_idx = pl.program_id(1)

    @pl.when(c_idx == 0)
    def _():
        ubuf[:, 0:SUBLANES, :] = jnp.zeros((SSD_N_GROUPS, SUBLANES, SSD_XBC_GW), F32)
        st_scr[...] = jnp.zeros(st_scr.shape, F32)

    row = lax.broadcasted_iota(I32, (q, q), 0)
    col = lax.broadcasted_iota(I32, (q, q), 1)
    causal = row >= col
    tril16 = causal.astype(F32).astype(BF16)
    lo = lax.broadcasted_iota(I32, (q, LANES), 1) < SSD_HEAD_DIM

    def pairx(v, j0):
        r = v.shape[0]
        return jnp.where(lo[:r], v[:, j0:j0 + 1], v[:, j0 + 1:j0 + 2])

    def group(g, carry):
        ubuf[g, SUBLANES:SUBLANES + q, :] = xbc_ref[g]
        w = cw_ref[g]
        conv = cb_ref[g]
        for k in range(SSD_CONV):
            conv = conv + ubuf[g, pl.ds(SUBLANES - (SSD_CONV - 1) + k, q), :] * w[k:k + 1, :]
        ubuf[g, 0:SUBLANES, :] = ubuf[g, q:q + SUBLANES, :]
        v = conv * jax.nn.sigmoid(conv)
        xs = v[:, :SSD_GW]
        bm = v[:, SSD_GW:SSD_GW + SSD_D_STATE]
        cm = v[:, SSD_GW + SSD_D_STATE:]

        dtr = dt_ref[g] + dtb_ref[g]
        dt = jnp.maximum(dtr, 0.0) + jnp.log1p(jnp.exp(-jnp.abs(dtr)))
        da = dt * (-jnp.exp(alog_ref[g]))
        h1, h2, h3 = _split3_bf16(da)
        cs = (jnp.dot(tril16, h1, preferred_element_type=F32)
              + jnp.dot(tril16, h2, preferred_element_type=F32)
              + jnp.dot(tril16, h3, preferred_element_type=F32))
        cs_t = cs.T
        last = cs[q - 1:q, :]
        ecs = jnp.exp(cs)
        wend = jnp.exp(last - cs)
        elast = jnp.exp(last)

        bm16 = bm.astype(BF16)
        cm16 = cm.astype(BF16)
        cb = lax.dot_general(cm16, bm16, _NT, preferred_element_type=F32)
        st_t = st_scr[g]
        ystate = jnp.dot(cm16, st_t.astype(BF16), preferred_element_type=F32)
        dsk = d_ref[g]

        ys, xws, els = [], [], []
        for p in range(SSD_HPG // 2):
            j0 = 2 * p
            xs_p = xs[:, p * LANES:(p + 1) * LANES]
            xdt = xs_p * pairx(dt, j0)
            lmats = []
            for j in (j0, j0 + 1):
                diff = cs[:, j:j + 1] - cs_t[j:j + 1, :]
                dec = jnp.exp(jnp.where(causal, diff, -jnp.inf))
                lmats.append((cb * dec).astype(BF16))
            lmat = jnp.concatenate(lmats, axis=1)
            rhs = jnp.concatenate([jnp.where(lo, xdt, 0.0), jnp.where(lo, 0.0, xdt)],
                                  axis=0).astype(BF16)
            y = jnp.dot(lmat, rhs, preferred_element_type=F32)
            y = y + ystate[:, p * LANES:(p + 1) * LANES] * pairx(ecs, j0)
            y = y + pairx(dsk, j0) * xs_p
            ys.append(y)
            xws.append((xdt * pairx(wend, j0)).astype(BF16))
            els.append(pairx(elast, j0))
        y = jnp.concatenate(ys, axis=1)
        xw = jnp.concatenate(xws, axis=1)
        el = jnp.concatenate(els, axis=1)
        st_scr[g] = st_t * el + jnp.dot(bm.T.astype(BF16), xw, preferred_element_type=F32)

        zz = z_ref[g]
        y = y * (zz * jax.nn.sigmoid(zz))
        y = y * lax.rsqrt(jnp.mean(y * y, axis=-1, keepdims=True) + NORM_EPS) * ng_ref[g]
        y_ref[g] = y.astype(y_ref.dtype)
        return carry

    lax.fori_loop(0, SSD_N_GROUPS, group, 0)


def ssd_mixer(z_g, xbc_g, dt_g, cw_g, cb_g, dtb_g, alog_g, d_g, ng_g, *, bsz, s_len):
    g, q = SSD_N_GROUPS, SSD_CHUNK
    m = bsz * s_len
    nc = s_len // q

    def act(w):
        return pl.BlockSpec((g, q, w), lambda b, c: (0, b * nc + c, 0))

    def par(r, w):
        return pl.BlockSpec((g, r, w), lambda b, c: (0, 0, 0))

    return pl.pallas_call(
        _ssd_kernel,
        grid=(bsz, nc),
        in_specs=[act(SSD_GW), act(SSD_XBC_GW), act(LANES),
                  par(SSD_CONV, SSD_XBC_GW), par(1, SSD_XBC_GW),
                  par(1, LANES), par(1, LANES), par(1, LANES), par(1, SSD_GW)],
        out_specs=act(SSD_GW),
        out_shape=jax.ShapeDtypeStruct((g, m, SSD_GW), BF16),
        scratch_shapes=[pltpu.VMEM((g, q + SUBLANES, SSD_XBC_GW), F32),
                        pltpu.VMEM((g, SSD_D_STATE, SSD_GW), F32)],
        compiler_params=_cparams(("arbitrary", "arbitrary")),
        name="ssd_mixer",
    )(z_g, xbc_g, dt_g, cw_g, cb_g, dtb_g, alog_g, d_g, ng_g)


def _dsa_kernel(q_ref, iq_ref, smq_ref, k_ref, v_ref, smk_ref, o_ref,
                key_scr, s_scr, mx_scr, l_scr, acc_scr, *, tq, ck, topk):
    i = pl.program_id(1)
    q0 = i * tq
    nch = lax.div(q0 + tq + ck - 1, ck)
    idx_scale = (IDX_HEAD_DIM ** -0.5) * (IDX_N_HEADS ** -0.5)
    att_scale = ATT_HEAD_DIM ** -0.5

    iw = smq_ref[0][:, IDX_HEAD_DIM:IDX_HEAD_DIM + IDX_N_HEADS]
    tpos = q0 + lax.broadcasted_iota(I32, (tq, ck), 0)
    lane_pos = lax.broadcasted_iota(I32, (tq, ck), 1)

    def score_chunk(c, carry):
        s0 = pl.multiple_of(c * ck, ck)
        ikc = smk_ref[0, pl.ds(s0, ck), :][:, :IDX_HEAD_DIM].astype(BF16)
        acc = jnp.zeros((tq, ck), F32)
        for h in range(IDX_N_HEADS):
            iqh = iq_ref[0, :, h * IDX_HEAD_DIM:(h + 1) * IDX_HEAD_DIM]
            d = lax.dot_general(iqh, ikc, _NT, preferred_element_type=F32)
            acc = acc + jnp.maximum(d, 0.0) * iw[:, h:h + 1]
        acc = acc * idx_scale
        bits = pltpu.bitcast(acc, I32)
        key = jnp.where(bits < 0, bits ^ jnp.int32(0x7FFFFFFF), bits)
        key_scr[c] = jnp.where(s0 + lane_pos <= tpos, key, jnp.int32(INT_MIN))
        return carry

    lax.fori_loop(0, nch, score_chunk, 0)

    def count_ge(cand):
        cand_b = jnp.broadcast_to(cand, (tq, LANES))

        def body(c, cnt):
            kc = key_scr[c]
            for u in range(ck // LANES):
                cnt = cnt + (kc[:, u * LANES:(u + 1) * LANES] >= cand_b).astype(I32)
            return cnt

        cnt = lax.fori_loop(0, nch, body, jnp.zeros((tq, LANES), I32))
        return jnp.sum(cnt.astype(F32), axis=1, keepdims=True)

    keep_all = q0 + lax.broadcasted_iota(I32, (tq, 1), 0) < topk

    def bis_cond(st):
        it, _, _, pending = st
        return jnp.logical_and(it < 32, pending > 0.0)

    def bis_step(st):
        it, thr, cnt_thr, _ = st
        cand = thr + lax.shift_left(jnp.int32(1), 31 - it)
        cnt = count_ge(cand)
        take = cnt >= float(topk)
        thr = jnp.where(take, cand, thr)
        cnt_thr = jnp.where(take, cnt, cnt_thr)
        done = jnp.logical_or(cnt_thr == float(topk), keep_all)
        return it + 1, thr, cnt_thr, jnp.sum(jnp.where(done, 0.0, 1.0))

    _, thr, _, _ = lax.while_loop(
        bis_cond, bis_step,
        (jnp.int32(0), jnp.full((tq, 1), INT_MIN, I32), jnp.full((tq, 1), 2.0 * topk, F32),
         jnp.float32(1.0)))
    thr = jnp.where(keep_all, jnp.int32(INT_MIN + 1), jnp.maximum(thr, jnp.int32(INT_MIN + 1)))
    thr_b = jnp.broadcast_to(thr, (tq, LANES))

    rows = ATT_GROUP * tq
    exp2_scale = att_scale * math.log2(math.e)

    def stacked_q(g):
        return jnp.concatenate(
            [q_ref[0, :, (g * ATT_GROUP + hh) * ATT_HEAD_DIM:(g * ATT_GROUP + hh + 1) * ATT_HEAD_DIM]
             for hh in range(ATT_GROUP)], axis=0)

    def logits_pass(g, qg, c):
        s0 = pl.multiple_of(c * ck, ck)
        ksl = slice(g * ATT_HEAD_DIM, (g + 1) * ATT_HEAD_DIM)
        s = lax.dot_general(qg, k_ref[0, pl.ds(s0, ck), ksl], _NT,
                            preferred_element_type=F32)
        kc = key_scr[c]
        sel = [kc[:, u * LANES:(u + 1) * LANES] >= thr_b for u in range(ck // LANES)]
        for hh in range(ATT_GROUP):
            rsl = slice(hh * tq, (hh + 1) * tq)
            mx = mx_scr[g % 2, rsl, :]
            for u in range(ck // LANES):
                lsl = slice(u * LANES, (u + 1) * LANES)
                sm = jnp.where(sel[u], s[rsl, lsl], NEG_BIG)
                s_scr[g % 2, c, rsl, lsl] = sm
                mx = jnp.maximum(mx, sm)
            mx_scr[g % 2, rsl, :] = mx

    def pv_pass(g, m_b, c):
        s0 = pl.multiple_of(c * ck, ck)
        ksl = slice(g * ATT_HEAD_DIM, (g + 1) * ATT_HEAD_DIM)
        lsum = l_scr[...]
        ps = []
        for u in range(ck // LANES):
            p = jnp.exp2((s_scr[g % 2, c, :, u * LANES:(u + 1) * LANES] - m_b) * exp2_scale)
            lsum = lsum + p
            ps.append(p.astype(BF16))
        l_scr[...] = lsum
        acc_scr[...] += jnp.dot(jnp.concatenate(ps, axis=1), v_ref[0, pl.ds(s0, ck), ksl],
                                preferred_element_type=F32)

    mx_scr[...] = jnp.full(mx_scr.shape, NEG_BIG, F32)
    q_cur = stacked_q(0)

    def first_pass(c, carry):
        logits_pass(0, q_cur, c)
        return carry

    lax.fori_loop(0, nch, first_pass, 0)
    for g in range(ATT_N_KV):
        m_b = jnp.broadcast_to(jnp.max(mx_scr[g % 2], axis=1, keepdims=True), (rows, LANES))
        l_scr[...] = jnp.zeros(l_scr.shape, F32)
        acc_scr[...] = jnp.zeros(acc_scr.shape, F32)
        if g + 1 < ATT_N_KV:
            mx_scr[(g + 1) % 2] = jnp.full((rows, LANES), NEG_BIG, F32)
            q_nxt = stacked_q(g + 1)

            def fused(c, carry, g=g, m_b=m_b, q_nxt=q_nxt):
                pv_pass(g, m_b, c)
                logits_pass(g + 1, q_nxt, c)
                return carry
        else:
            def fused(c, carry, g=g, m_b=m_b):
                pv_pass(g, m_b, c)
                return carry

        lax.fori_loop(0, nch, fused, 0)
        o = acc_scr[...] / jnp.sum(l_scr[...], axis=1, keepdims=True)
        for hh in range(ATT_GROUP):
            h = g * ATT_GROUP + hh
            o_ref[0, :, h * ATT_HEAD_DIM:(h + 1) * ATT_HEAD_DIM] = (
                o[hh * tq:(hh + 1) * tq]).astype(o_ref.dtype)


def dsa_mixer(q, k, v, iq, small, *, bsz, s_len, tq=128, ck=512):
    topk = min(TOPK_MAX, s_len // 4)
    ck = min(ck, s_len)
    assert ck >= topk and s_len % ck == 0 and s_len % tq == 0
    nq = s_len // tq
    hq = ATT_N_HEADS * ATT_HEAD_DIM
    hk = ATT_N_KV * ATT_HEAD_DIM
    hi = IDX_N_HEADS * IDX_HEAD_DIM
    r3 = lambda a: a.reshape(bsz, s_len, a.shape[-1])
    out = pl.pallas_call(
        functools.partial(_dsa_kernel, tq=tq, ck=ck, topk=topk),
        grid=(bsz, nq),
        in_specs=[pl.BlockSpec((1, tq, hq), lambda b, i: (b, i, 0)),
                  pl.BlockSpec((1, tq, hi), lambda b, i: (b, i, 0)),
                  pl.BlockSpec((1, tq, LANES), lambda b, i: (b, i, 0)),
                  pl.BlockSpec((1, s_len, hk), lambda b, i: (b, 0, 0)),
                  pl.BlockSpec((1, s_len, hk), lambda b, i: (b, 0, 0)),
                  pl.BlockSpec((1, s_len, LANES), lambda b, i: (b, 0, 0))],
        out_specs=pl.BlockSpec((1, tq, hq), lambda b, i: (b, i, 0)),
        out_shape=jax.ShapeDtypeStruct((bsz, s_len, hq), BF16),
        scratch_shapes=[pltpu.VMEM((s_len // ck, tq, ck), I32),
                        pltpu.VMEM((2, s_len // ck, ATT_GROUP * tq, ck), F32),
                        pltpu.VMEM((2, ATT_GROUP * tq, LANES), F32),
                        pltpu.VMEM((ATT_GROUP * tq, LANES), F32),
                        pltpu.VMEM((ATT_GROUP * tq, ATT_HEAD_DIM), F32)],
        compiler_params=_cparams(("parallel", "arbitrary")),
        name="dsa_mixer",
    )(r3(q), r3(iq), r3(small), r3(k), r3(v), r3(small))
    return out.reshape(bsz * s_len, hq)


def _memattn_kernel(q_ref, kv_ref, o_ref):
    scale = MEM_HEAD_DIM ** -0.5
    for h in range(MEM_HEADS):
        sl = slice(h * MEM_HEAD_DIM, (h + 1) * MEM_HEAD_DIM)
        qh = q_ref[:, sl]
        kh = kv_ref[:, sl]
        vh = kv_ref[:, D_MODEL + h * MEM_HEAD_DIM:D_MODEL + (h + 1) * MEM_HEAD_DIM]
        s = lax.dot_general(qh, kh, _NT, preferred_element_type=F32) * scale
        p = jnp.exp(s - jnp.max(s, axis=1, keepdims=True))
        o = jnp.dot(p.astype(BF16), vh, preferred_element_type=F32)
        o_ref[:, sl] = (o / jnp.sum(p, axis=1, keepdims=True)).astype(o_ref.dtype)


def mem_attention(qm, kv, *, bsz, s_len, tq=512):
    m = bsz * s_len
    tq = min(tq, s_len)
    nq = s_len // tq
    n_mem = kv.shape[0] // bsz
    return pl.pallas_call(
        _memattn_kernel,
        grid=(bsz, nq),
        in_specs=[pl.BlockSpec((tq, D_MODEL), lambda b, i: (b * nq + i, 0)),
                  pl.BlockSpec((n_mem, 2 * D_MODEL), lambda b, i: (b, 0))],
        out_specs=pl.BlockSpec((tq, D_MODEL), lambda b, i: (b * nq + i, 0)),
        out_shape=jax.ShapeDtypeStruct((m, D_MODEL), BF16),
        compiler_params=_cparams(("parallel", "parallel")),
        name="mem_attention",
    )(qm, kv)


def _rope_tables(positions):
    m = positions.size
    pos = positions.astype(F32).reshape(m, 1)

    def unit(head_dim):
        rot = head_dim // ROPE_FRACTION
        half = rot // 2
        inv_freq = jnp.power(ROPE_THETA, -jnp.arange(half, dtype=F32) * 2.0 / rot)
        ang = pos * inv_freq
        cos, sin = jnp.cos(ang), jnp.sin(ang)
        one = jnp.ones((m, head_dim - rot), F32)
        zero = jnp.zeros((m, head_dim - rot), F32)
        zh = jnp.zeros((m, half), F32)
        c = jnp.concatenate([cos, cos, one], axis=1)
        s1 = jnp.concatenate([zh, sin, zero], axis=1)
        s2 = jnp.concatenate([-sin, zh, zero], axis=1)
        return c, s1, s2

    att = unit(ATT_HEAD_DIM)
    ic, is1, is2 = unit(IDX_HEAD_DIM)
    idx = tuple(jnp.concatenate([t, t], axis=1) for t in (ic, is1, is2))
    pad1 = jnp.ones((m, LANES - IDX_HEAD_DIM), F32)
    pad0 = jnp.zeros((m, LANES - IDX_HEAD_DIM), F32)
    small = (jnp.concatenate([ic, pad1], axis=1),
             jnp.concatenate([is1, pad0], axis=1),
             jnp.concatenate([is2, pad0], axis=1))
    return att, idx, small


def _prep_in_proj(w):
    offs = [0]
    for s in IN_PROJ_SIZES:
        offs.append(offs[-1] + s)
    seg = lambda i: w[:, offs[i]:offs[i + 1]]
    z, xbc, dt, q, k, v, iq, ik, iw, g_ssd, g_att = (seg(i) for i in range(11))
    d = w.shape[0]
    g = SSD_N_GROUPS
    xs_w = xbc[:, :SSD_D_INNER].reshape(d, g, SSD_GW)
    b_w = xbc[:, SSD_D_INNER:SSD_D_INNER + g * SSD_D_STATE].reshape(d, g, SSD_D_STATE)
    c_w = xbc[:, SSD_D_INNER + g * SSD_D_STATE:].reshape(d, g, SSD_D_STATE)
    xbc_gm = jnp.concatenate([xs_w, b_w, c_w], axis=2).reshape(d, g * SSD_XBC_GW)
    dt_pad = jnp.pad(dt.reshape(d, g, SSD_HPG), ((0, 0), (0, 0), (0, LANES - SSD_HPG)))
    small = jnp.concatenate(
        [ik, iw, jnp.zeros((d, LANES - IDX_HEAD_DIM - IDX_N_HEADS), w.dtype)], axis=1)
    c16 = lambda a: a.astype(BF16)
    return dict(z=c16(z), xbc=c16(xbc_gm), dt=c16(dt_pad.reshape(d, g * LANES)),
                q=c16(q), k=c16(k), v=c16(v), iq=c16(iq), small=c16(small),
                gates=c16(jnp.concatenate([g_ssd, g_att], axis=1)))


def _prep_ssd_params(conv_w, conv_b, dt_bias, a_log, d_skip, norm_g):
    g = SSD_N_GROUPS

    def gm_channels(a):
        r = a.shape[0]
        xs = a[:, :SSD_D_INNER].reshape(r, g, SSD_GW)
        b = a[:, SSD_D_INNER:SSD_D_INNER + g * SSD_D_STATE].reshape(r, g, SSD_D_STATE)
        c = a[:, SSD_D_INNER + g * SSD_D_STATE:].reshape(r, g, SSD_D_STATE)
        return jnp.transpose(jnp.concatenate([xs, b, c], axis=2), (1, 0, 2))

    def gm_heads(a):
        return jnp.pad(a.reshape(g, 1, SSD_HPG), ((0, 0), (0, 0), (0, LANES - SSD_HPG)))

    return (gm_channels(conv_w), gm_channels(conv_b[None, :]), gm_heads(dt_bias),
            gm_heads(a_log), gm_heads(d_skip), norm_g.reshape(g, 1, SSD_GW))


def kernel(x, mem, positions, norm_mix_pre, norm_mix_post, w_in, ssd_conv_w, ssd_conv_b,
           ssd_dt_bias, ssd_a_log, ssd_d, ssd_norm, w_br_ssd, w_br_att, w_out,
           norm_mem_pre, norm_mem_post, mem_norm, w_mem_q, w_mem_kv, w_mem_o,
           norm_ffn_pre, norm_ffn_post, w_ffn_in, w_ffn_out):
    bsz, s_len, d = x.shape
    m = bsz * s_len
    depth = w_in.shape[0]
    x2 = x.reshape(m, d)
    mem2 = mem.reshape(bsz * mem.shape[1], d)
    rope_att, rope_idx, rope_small = _rope_tables(positions)
    rope_ex = lambda t: tuple((a, "m128") for a in t)
    tm = 1024
    gcol = D_MODEL // 1024

    for l in range(depth):
        wp = _prep_in_proj(w_in[l])
        hn = rmsnorm_bf16(x2, norm_mix_pre[l])
        z_g = matmul(hn, wp["z"], tm=tm, tn=SSD_GW, out_gm=True, name="proj_z")
        xbc_g = matmul(hn, wp["xbc"], tm=tm, tn=SSD_XBC_GW, out_gm=True, name="proj_xbc")
        dt_g = matmul(hn, wp["dt"], tm=tm, tn=LANES, out_gm=True, name="proj_dt")
        q = matmul(hn, wp["q"], tm=tm, tn=1024, out_dtype=BF16, name="proj_q",
                   epi=functools.partial(_epi_rope, ATT_HEAD_DIM // ROPE_FRACTION // 2, 1024 // LANES),
                   extras=rope_ex(rope_att))
        k = matmul(hn, wp["k"], tm=tm, tn=512, out_dtype=BF16, name="proj_k",
                   epi=functools.partial(_epi_rope, ATT_HEAD_DIM // ROPE_FRACTION // 2, 512 // LANES),
                   extras=rope_ex(rope_att))
        v = matmul(hn, wp["v"], tm=tm, tn=512, out_dtype=BF16, name="proj_v")
        iq = matmul(hn, wp["iq"], tm=tm, tn=1024, out_dtype=BF16, name="proj_iq",
                    epi=functools.partial(_epi_rope, IDX_HEAD_DIM // ROPE_FRACTION // 2, 1024 // LANES),
                    extras=rope_ex(rope_idx))
        small = matmul(hn, wp["small"], tm=tm, tn=LANES, name="proj_small",
                       epi=functools.partial(_epi_rope, IDX_HEAD_DIM // ROPE_FRACTION // 2, 1),
                       extras=rope_ex(rope_small))
        gates = matmul(hn, wp["gates"], tm=tm, tn=1024, epi=_epi_sigmoid, name="proj_gates")

        ssd_par = _prep_ssd_params(ssd_conv_w[l], ssd_conv_b[l], ssd_dt_bias[l],
                                   ssd_a_log[l], ssd_d[l], ssd_norm[l])
        y_g = ssd_mixer(z_g, xbc_g, dt_g, *ssd_par, bsz=bsz, s_len=s_len)
        att = dsa_mixer(q, k, v, iq, small, bsz=bsz, s_len=s_len)

        u_ssd = matmul(y_g, w_br_ssd[l].astype(BF16), tm=tm, tn=1024, a_gm=True,
                       epi=_epi_gate_mul, extras=((gates, "mn", 0),), name="branch_ssd")
        merged = matmul(att, w_br_att[l].astype(BF16), tm=tm, tn=1024, out_dtype=BF16,
                        epi=_epi_gate_mul_add, extras=((gates, "mn", gcol), (u_ssd, "mn", 0)),
                        name="branch_att_merge")
        x2 = matmul(merged, w_out[l].astype(BF16), tm=256, tn=D_MODEL, epi=_epi_norm_res,
                    extras=((norm_mix_post[l].reshape(1, d), "n"), (x2, "mn")), name="mix_out")

        hn = rmsnorm_bf16(x2, norm_mem_pre[l])
        qm = matmul(hn, w_mem_q[l].astype(BF16), tm=tm, tn=1024, out_dtype=BF16, name="mem_q")
        mem_n = rmsnorm_bf16(mem2, mem_norm[l])
        kvm = matmul(mem_n, w_mem_kv[l].astype(BF16), tm=tm, tn=1024, out_dtype=BF16, name="mem_kv")
        om = mem_attention(qm, kvm, bsz=bsz, s_len=s_len)
        x2 = matmul(om, w_mem_o[l].astype(BF16), tm=256, tn=D_MODEL, epi=_epi_norm_res,
                    extras=((norm_mem_post[l].reshape(1, d), "n"), (x2, "mn")), name="mem_out")

        hn = rmsnorm_bf16(x2, norm_ffn_pre[l])
        wf = w_ffn_in[l]
        hf = swiglu_in(hn, wf[:, :D_FF].astype(BF16), wf[:, D_FF:].astype(BF16), tm=tm, tn=512)
        x2 = matmul(hf, w_ffn_out[l].astype(BF16), tm=512, tn=D_MODEL, tk=D_FF // 4,
                    epi=_epi_norm_res,
                    extras=((norm_ffn_post[l].reshape(1, d), "n"), (x2, "mn")), name="ffn_out")

    return x2.reshape(bsz, s_len, d)
```

```python
import functools
import math

import jax
import jax.numpy as jnp
from jax import lax
from jax.experimental import pallas as pl
from jax.experimental.pallas import tpu as pltpu

F32 = jnp.float32
BF16 = jnp.bfloat16
I32 = jnp.int32

D_MODEL = 2048
N_MEM = 256
SSD_D_INNER = 4096
SSD_HEAD_DIM = 64
SSD_N_HEADS = 64
SSD_N_GROUPS = 8
SSD_HPG = SSD_N_HEADS // SSD_N_GROUPS
SSD_GW = SSD_D_INNER // SSD_N_GROUPS
SSD_D_STATE = 128
SSD_CONV = 4
SSD_CHUNK = 128
SSD_XBC_GW = SSD_GW + 2 * SSD_D_STATE
ATT_HEAD_DIM = 128
ATT_N_HEADS = 16
ATT_N_KV = 4
ATT_GROUP = ATT_N_HEADS // ATT_N_KV
IDX_N_HEADS = 16
IDX_HEAD_DIM = 64
TOPK_MAX = 256
ROPE_THETA = 500000.0
ROPE_FRACTION = 4
MEM_HEADS = 4
MEM_HEAD_DIM = D_MODEL // MEM_HEADS
D_FF = 5632
NORM_EPS = 1e-6
IN_PROJ_SIZES = (4096, 6144, 64, 2048, 512, 512, 1024, 64, 16, 2048, 2048)

LANES = 128
SUBLANES = 8
V7X_VMEM_BYTES = 64 * 1024 * 1024
VMEM_LIMIT = V7X_VMEM_BYTES - 8 * 1024 * 1024

INT_MIN = -(2 ** 31)
NEG_BIG = -1e30

_NT = (((1,), (1,)), ((), ()))


def _cparams(sem):
    return pltpu.CompilerParams(dimension_semantics=sem, vmem_limit_bytes=VMEM_LIMIT)


def _rmsnorm_kernel(x_ref, g_ref, o_ref):
    x = x_ref[...]
    ms = jnp.mean(x * x, axis=-1, keepdims=True)
    o_ref[...] = (x * lax.rsqrt(ms + NORM_EPS) * g_ref[...]).astype(o_ref.dtype)


def rmsnorm_bf16(x2, g, tm=256):
    m, d = x2.shape
    tm = min(tm, m)
    return pl.pallas_call(
        _rmsnorm_kernel,
        grid=(m // tm,),
        in_specs=[pl.BlockSpec((tm, d), lambda i: (i, 0)),
                  pl.BlockSpec((1, d), lambda i: (0, 0))],
        out_specs=pl.BlockSpec((tm, d), lambda i: (i, 0)),
        out_shape=jax.ShapeDtypeStruct((m, d), BF16),
        compiler_params=_cparams(("parallel",)),
        name="rmsnorm",
    )(x2, g.reshape(1, d))


def _epi_none(acc):
    return acc


def _epi_sigmoid(acc):
    return jax.nn.sigmoid(acc)


def _epi_gate_mul(acc, gate_ref):
    return acc * gate_ref[...].astype(F32)


def _epi_gate_mul_add(acc, gate_ref, add_ref):
    return acc * gate_ref[...].astype(F32) + add_ref[...].astype(F32)


def _epi_rope(shift, n_rot, acc, c_ref, s1_ref, s2_ref):
    c = c_ref[...]
    s1 = s1_ref[...]
    s2 = s2_ref[...]
    outs = []
    for j in range(acc.shape[1] // LANES):
        sl = acc[:, j * LANES:(j + 1) * LANES]
        if j < n_rot:
            sl = (sl * c + pltpu.roll(sl, shift, 1) * s1
                  + pltpu.roll(sl, LANES - shift, 1) * s2)
        outs.append(sl)
    return outs[0] if len(outs) == 1 else jnp.concatenate(outs, axis=1)


def _epi_norm_res(acc, g_ref, x_ref):
    ms = jnp.mean(acc * acc, axis=-1, keepdims=True)
    return x_ref[...] + acc * lax.rsqrt(ms + NORM_EPS) * g_ref[...]


def _epi_norm_res_next(acc, g_ref, x_ref, gnext_ref):
    xn = _epi_norm_res(acc, g_ref, x_ref)
    ms = jnp.mean(xn * xn, axis=-1, keepdims=True)
    return xn, xn * lax.rsqrt(ms + NORM_EPS) * gnext_ref[...]


def _mm_kernel(epi, n_extra, n_out, nk, a_ref, w_ref, *refs):
    extras = refs[:n_extra]
    o_refs = refs[n_extra:n_extra + n_out]

    def finish(acc):
        vals = epi(acc, *extras)
        vals = vals if isinstance(vals, tuple) else (vals,)
        for o_ref, val in zip(o_refs, vals, strict=True):
            o_ref[...] = val.astype(o_ref.dtype).reshape(o_ref.shape)

    a = a_ref[...]
    a = a.reshape(a.shape[-2], a.shape[-1])
    part = jnp.dot(a, w_ref[...], preferred_element_type=F32)
    if nk == 1:
        finish(part)
        return
    acc_ref = refs[n_extra + n_out]
    k = pl.program_id(2)

    @pl.when(k == 0)
    def _():
        acc_ref[...] = part

    @pl.when(k > 0)
    def _():
        acc_ref[...] += part

    @pl.when(k == nk - 1)
    def _():
        finish(acc_ref[...])


def matmul(a, w, *, tm, tn, tk=None, out_dtype=F32, epi=_epi_none, extras=(),
           a_gm=False, out_gm=False, out2_dtype=None, name="matmul"):
    kdim, n = w.shape
    if a_gm:
        nk, m, tk = a.shape
        assert nk * tk == kdim
    else:
        m = a.shape[0]
        tk = kdim if tk is None else tk
        assert kdim % tk == 0
        nk = kdim // tk
    tm = min(tm, m)
    assert m % tm == 0 and n % tn == 0
    if a_gm:
        a_spec = pl.BlockSpec((1, tm, tk), lambda i, j, k: (k, i, 0))
    else:
        a_spec = pl.BlockSpec((tm, tk), lambda i, j, k: (i, k))
    in_specs = [a_spec, pl.BlockSpec((tk, tn), lambda i, j, k: (k, j))]
    args = [a, w]
    for ex in extras:
        arr, kind = ex[0], ex[1]
        off = ex[2] if len(ex) > 2 else 0
        if kind == "mn":
            in_specs.append(pl.BlockSpec((tm, tn), lambda i, j, k, off=off: (i, j + off)))
        elif kind == "m128":
            in_specs.append(pl.BlockSpec((tm, LANES), lambda i, j, k: (i, 0)))
        elif kind == "n":
            in_specs.append(pl.BlockSpec((1, tn), lambda i, j, k: (0, j)))
        else:
            raise ValueError(kind)
        args.append(arr)
    if out_gm:
        out_spec = pl.BlockSpec((1, tm, tn), lambda i, j, k: (j, i, 0))
        out_shape = jax.ShapeDtypeStruct((n // tn, m, tn), out_dtype)
    else:
        out_spec = pl.BlockSpec((tm, tn), lambda i, j, k: (i, j))
        out_shape = jax.ShapeDtypeStruct((m, n), out_dtype)
    n_out = 1
    if out2_dtype is not None:
        assert not out_gm
        n_out = 2
        out_spec = [out_spec, pl.BlockSpec((tm, tn), lambda i, j, k: (i, j))]
        out_shape = [out_shape, jax.ShapeDtypeStruct((m, n), out2_dtype)]
    scratch = [pltpu.VMEM((tm, tn), F32)] if nk > 1 else []
    return pl.pallas_call(
        functools.partial(_mm_kernel, epi, len(extras), n_out, nk),
        grid=(m // tm, n // tn, nk),
        in_specs=in_specs,
        out_specs=out_spec,
        out_shape=out_shape,
        scratch_shapes=scratch,
        compiler_params=_cparams(("parallel", "parallel", "arbitrary")),
        name=name,
    )(*args)


def _swiglu_kernel(a_ref, wg_ref, wu_ref, o_ref):
    a = a_ref[...]
    g = jnp.dot(a, wg_ref[...], preferred_element_type=F32)
    u = jnp.dot(a, wu_ref[...], preferred_element_type=F32)
    o_ref[...] = (g * jax.nn.sigmoid(g) * u).astype(o_ref.dtype)


def swiglu_in(a, wg, wu, *, tm, tn):
    m, kdim = a.shape
    n = wg.shape[1]
    tm = min(tm, m)
    return pl.pallas_call(
        _swiglu_kernel,
        grid=(m // tm, n // tn),
        in_specs=[pl.BlockSpec((tm, kdim), lambda i, j: (i, 0)),
                  pl.BlockSpec((kdim, tn), lambda i, j: (0, j)),
                  pl.BlockSpec((kdim, tn), lambda i, j: (0, j))],
        out_specs=pl.BlockSpec((tm, tn), lambda i, j: (i, j)),
        out_shape=jax.ShapeDtypeStruct((m, n), BF16),
        compiler_params=_cparams(("parallel", "parallel")),
        name="swiglu_in",
    )(a, wg, wu)


def _split3_bf16(x):
    h1 = x.astype(BF16)
    r1 = x - h1.astype(F32)
    h2 = r1.astype(BF16)
    h3 = (r1 - h2.astype(F32)).astype(BF16)
    return h1, h2, h3


def _ssd_kernel(z_ref, xbc_ref, dt_ref, cw_ref, cb_ref, dtb_ref, alog_ref, d_ref, ng_ref,
                y_ref, ubuf, st_scr, cst_scr):
    q = SSD_CHUNK
    c_idx = pl.program_id(1)

    @pl.when(c_idx == 0)
    def _():
        ubuf[:, 0:SUBLANES, :] = jnp.zeros((SSD_N_GROUPS, SUBLANES, SSD_XBC_GW), F32)
        st_scr[...] = jnp.zeros(st_scr.shape, F32)

    row = lax.broadcasted_iota(I32, (q, q), 0)
    col = lax.broadcasted_iota(I32, (q, q), 1)
    causal = row >= col
    tril16 = causal.astype(F32).astype(BF16)
    lo = lax.broadcasted_iota(I32, (q, LANES), 1) < SSD_HEAD_DIM

    def pairx(v, j0):
        r = v.shape[0]
        return jnp.where(lo[:r], v[:, j0:j0 + 1], v[:, j0 + 1:j0 + 2])

    dtr = dt_ref[...] + dtb_ref[...]
    dt_all = jnp.maximum(dtr, 0.0) + jnp.log1p(jnp.exp(-jnp.abs(dtr)))
    da = dt_all * (-jnp.exp(alog_ref[...]))
    h1, h2, h3 = _split3_bf16(da)
    cs_all = (jnp.dot(tril16, h1, preferred_element_type=F32)
              + jnp.dot(tril16, h2, preferred_element_type=F32)
              + jnp.dot(tril16, h3, preferred_element_type=F32))
    cst_scr[...] = cs_all.T
    last_all = cs_all[q - 1:q, :]
    ecs_all = jnp.exp(cs_all)
    wend_all = jnp.exp(last_all - cs_all)
    elast_all = jnp.exp(last_all)
    dsk_all = d_ref[...]

    def group(g, carry):
        shift = lax.rem(LANES - SSD_HPG * g, LANES)
        take = lambda v: pltpu.roll(v, shift, 1)
        dt, cs, ecs, wend = take(dt_all), take(cs_all), take(ecs_all), take(wend_all)
        elast = take(jnp.broadcast_to(elast_all, (SUBLANES, LANES)))[:1]
        dsk = take(jnp.broadcast_to(dsk_all, (SUBLANES, LANES)))[:1]
        cs_t = cst_scr[pl.ds(pl.multiple_of(g * SSD_HPG, SSD_HPG), SSD_HPG), :]

        ubuf[g, SUBLANES:SUBLANES + q, :] = xbc_ref[g]
        w = cw_ref[g]
        conv = cb_ref[g]
        for k in range(SSD_CONV):
            conv = conv + ubuf[g, pl.ds(SUBLANES - (SSD_CONV - 1) + k, q), :] * w[k:k + 1, :]
        ubuf[g, 0:SUBLANES, :] = ubuf[g, q:q + SUBLANES, :]
        v = conv * jax.nn.sigmoid(conv)
        xs = v[:, :SSD_GW]
        bm = v[:, SSD_GW:SSD_GW + SSD_D_STATE]
        cm = v[:, SSD_GW + SSD_D_STATE:]

        bm16 = bm.astype(BF16)
        cm16 = cm.astype(BF16)
        cb = lax.dot_general(cm16, bm16, _NT, preferred_element_type=F32)
        st_t = st_scr[g]
        ystate = jnp.dot(cm16, st_t.astype(BF16), preferred_element_type=F32)

        ys, xws, els = [], [], []
        for p in range(SSD_HPG // 2):
            j0 = 2 * p
            xs_p = xs[:, p * LANES:(p + 1) * LANES]
            xdt = xs_p * pairx(dt, j0)
            lmats = []
            for j in (j0, j0 + 1):
                diff = cs[:, j:j + 1] - cs_t[j:j + 1, :]
                dec = jnp.exp(jnp.where(causal, diff, -jnp.inf))
                lmats.append((cb * dec).astype(BF16))
            lmat = jnp.concatenate(lmats, axis=1)
            rhs = jnp.concatenate([jnp.where(lo, xdt, 0.0), jnp.where(lo, 0.0, xdt)],
                                  axis=0).astype(BF16)
            y = jnp.dot(lmat, rhs, preferred_element_type=F32)
            y = y + ystate[:, p * LANES:(p + 1) * LANES] * pairx(ecs, j0)
            y = y + pairx(dsk, j0) * xs_p
            ys.append(y)
            xws.append((xdt * pairx(wend, j0)).astype(BF16))
            els.append(pairx(elast, j0))
        y = jnp.concatenate(ys, axis=1)
        xw = jnp.concatenate(xws, axis=1)
        el = jnp.concatenate(els, axis=1)
        st_scr[g] = st_t * el + jnp.dot(bm.T.astype(BF16), xw, preferred_element_type=F32)

        zz = z_ref[g]
        y = y * (zz * jax.nn.sigmoid(zz))
        y = y * lax.rsqrt(jnp.mean(y * y, axis=-1, keepdims=True) + NORM_EPS) * ng_ref[g]
        y_ref[g] = y.astype(y_ref.dtype)
        return carry

    lax.fori_loop(0, SSD_N_GROUPS, group, 0)


def ssd_mixer(z_g, xbc_g, small, cw_g, cb_g, dtb, alog, dskip, ng_g, *, bsz, s_len):
    g, q = SSD_N_GROUPS, SSD_CHUNK
    m = bsz * s_len
    nc = s_len // q

    def act(w):
        return pl.BlockSpec((g, q, w), lambda b, c: (0, b * nc + c, 0))

    def par(r, w):
        return pl.BlockSpec((g, r, w), lambda b, c: (0, 0, 0))

    head = pl.BlockSpec((1, LANES), lambda b, c: (0, 0))

    return pl.pallas_call(
        _ssd_kernel,
        grid=(bsz, nc),
        in_specs=[act(SSD_GW), act(SSD_XBC_GW),
                  pl.BlockSpec((q, LANES), lambda b, c: (b * nc + c, 1)),
                  par(SSD_CONV, SSD_XBC_GW), par(1, SSD_XBC_GW),
                  head, head, head, par(1, SSD_GW)],
        out_specs=act(SSD_GW),
        out_shape=jax.ShapeDtypeStruct((g, m, SSD_GW), BF16),
        scratch_shapes=[pltpu.VMEM((g, q + SUBLANES, SSD_XBC_GW), F32),
                        pltpu.VMEM((g, SSD_D_STATE, SSD_GW), F32),
                        pltpu.VMEM((LANES, q), F32)],
        compiler_params=_cparams(("arbitrary", "arbitrary")),
        name="ssd_mixer",
    )(z_g, xbc_g, small, cw_g, cb_g, dtb, alog, dskip, ng_g)


def _dsa_kernel(q_ref, iq_ref, smq_ref, k_ref, v_ref, smk_ref, o_ref,
                key_scr, s_scr, mx_scr, l_scr, acc_scr, *, tq, ck, topk):
    i = pl.program_id(1)
    q0 = i * tq
    nch = lax.div(q0 + tq + ck - 1, ck)
    idx_scale = (IDX_HEAD_DIM ** -0.5) * (IDX_N_HEADS ** -0.5)
    att_scale = ATT_HEAD_DIM ** -0.5

    iw = smq_ref[0][:, IDX_HEAD_DIM:IDX_HEAD_DIM + IDX_N_HEADS]
    tpos = q0 + lax.broadcasted_iota(I32, (tq, ck), 0)
    lane_pos = lax.broadcasted_iota(I32, (tq, ck), 1)

    def score_chunk(c, carry):
        s0 = pl.multiple_of(c * ck, ck)
        ikc = smk_ref[0, pl.ds(s0, ck), :][:, :IDX_HEAD_DIM].astype(BF16)
        acc = jnp.zeros((tq, ck), F32)
        for h in range(IDX_N_HEADS):
            iqh = iq_ref[0, :, h * IDX_HEAD_DIM:(h + 1) * IDX_HEAD_DIM]
            d = lax.dot_general(iqh, ikc, _NT, preferred_element_type=F32)
            acc = acc + jnp.maximum(d, 0.0) * iw[:, h:h + 1]
        acc = acc * idx_scale
        bits = pltpu.bitcast(acc, I32)
        key = jnp.where(bits < 0, bits ^ jnp.int32(0x7FFFFFFF), bits)
        key_scr[c] = jnp.where(s0 + lane_pos <= tpos, key, jnp.int32(INT_MIN))
        return carry

    lax.fori_loop(0, nch, score_chunk, 0)

    def count_ge(cand):
        cand_b = jnp.broadcast_to(cand, (tq, LANES))

        def body(c, cnt):
            kc = key_scr[c]
            for u in range(ck // LANES):
                cnt = cnt + (kc[:, u * LANES:(u + 1) * LANES] >= cand_b).astype(I32)
            return cnt

        cnt = lax.fori_loop(0, nch, body, jnp.zeros((tq, LANES), I32))
        return jnp.sum(cnt.astype(F32), axis=1, keepdims=True)

    keep_all = q0 + lax.broadcasted_iota(I32, (tq, 1), 0) < topk

    def bis_cond(st):
        it, _, _, pending = st
        return jnp.logical_and(it < 32, pending > 0.0)

    def bis_step(st):
        it, thr, cnt_thr, _ = st
        cand = thr + lax.shift_left(jnp.int32(1), 31 - it)
        cnt = count_ge(cand)
        take = cnt >= float(topk)
        thr = jnp.where(take, cand, thr)
        cnt_thr = jnp.where(take, cnt, cnt_thr)
        done = jnp.logical_or(cnt_thr == float(topk), keep_all)
        return it + 1, thr, cnt_thr, jnp.sum(jnp.where(done, 0.0, 1.0))

    _, thr, _, _ = lax.while_loop(
        bis_cond, bis_step,
        (jnp.int32(0), jnp.full((tq, 1), INT_MIN, I32), jnp.full((tq, 1), 2.0 * topk, F32),
         jnp.float32(1.0)))
    thr = jnp.where(keep_all, jnp.int32(INT_MIN + 1), jnp.maximum(thr, jnp.int32(INT_MIN + 1)))
    thr_b = jnp.broadcast_to(thr, (tq, LANES))

    rows = ATT_GROUP * tq
    exp2_scale = att_scale * math.log2(math.e)

    def stacked_q(g):
        return jnp.concatenate(
            [q_ref[0, :, (g * ATT_GROUP + hh) * ATT_HEAD_DIM:(g * ATT_GROUP + hh + 1) * ATT_HEAD_DIM]
             for hh in range(ATT_GROUP)], axis=0)

    def logits_pass(g, qg, chunks):
        ksl = slice(g * ATT_HEAD_DIM, (g + 1) * ATT_HEAD_DIM)
        mxs = [mx_scr[g % 2, hh * tq:(hh + 1) * tq, :] for hh in range(ATT_GROUP)]
        for c in chunks:
            s0 = pl.multiple_of(c * ck, ck)
            s = lax.dot_general(qg, k_ref[0, pl.ds(s0, ck), ksl], _NT,
                                preferred_element_type=F32)
            kc = key_scr[c]
            sel = [kc[:, u * LANES:(u + 1) * LANES] >= thr_b for u in range(ck // LANES)]
            for hh in range(ATT_GROUP):
                rsl = slice(hh * tq, (hh + 1) * tq)
                for u in range(ck // LANES):
                    lsl = slice(u * LANES, (u + 1) * LANES)
                    sm = jnp.where(sel[u], s[rsl, lsl], NEG_BIG)
                    s_scr[g % 2, c, rsl, lsl] = sm
                    mxs[hh] = jnp.maximum(mxs[hh], sm)
        for hh in range(ATT_GROUP):
            mx_scr[g % 2, hh * tq:(hh + 1) * tq, :] = mxs[hh]

    def pv_pass(g, m_b, chunks):
        ksl = slice(g * ATT_HEAD_DIM, (g + 1) * ATT_HEAD_DIM)
        lsum = l_scr[...]
        acc = acc_scr[...]
        for c in chunks:
            s0 = pl.multiple_of(c * ck, ck)
            ps = []
            for u in range(ck // LANES):
                p = jnp.exp2((s_scr[g % 2, c, :, u * LANES:(u + 1) * LANES] - m_b) * exp2_scale)
                lsum = lsum + p
                ps.append(p.astype(BF16))
            acc = acc + jnp.dot(jnp.concatenate(ps, axis=1), v_ref[0, pl.ds(s0, ck), ksl],
                                preferred_element_type=F32)
        l_scr[...] = lsum
        acc_scr[...] = acc

    def chunk_loop(body):
        def pair(t, carry):
            body((2 * t, 2 * t + 1))
            return carry

        lax.fori_loop(0, lax.shift_right_logical(nch, 1), pair, 0)

        @pl.when(lax.rem(nch, 2) == 1)
        def _():
            body((nch - 1,))

    mx_scr[...] = jnp.full(mx_scr.shape, NEG_BIG, F32)
    q_cur = stacked_q(0)
    chunk_loop(lambda chunks: logits_pass(0, q_cur, chunks))
    for g in range(ATT_N_KV):
        m_b = jnp.broadcast_to(jnp.max(mx_scr[g % 2], axis=1, keepdims=True), (rows, LANES))
        l_scr[...] = jnp.zeros(l_scr.shape, F32)
        acc_scr[...] = jnp.zeros(acc_scr.shape, F32)
        if g + 1 < ATT_N_KV:
            mx_scr[(g + 1) % 2] = jnp.full((rows, LANES), NEG_BIG, F32)
            q_nxt = stacked_q(g + 1)

            def fused(chunks, g=g, m_b=m_b, q_nxt=q_nxt):
                pv_pass(g, m_b, chunks)
                logits_pass(g + 1, q_nxt, chunks)
        else:
            def fused(chunks, g=g, m_b=m_b):
                pv_pass(g, m_b, chunks)

        chunk_loop(fused)
        o = acc_scr[...] / jnp.sum(l_scr[...], axis=1, keepdims=True)
        for hh in range(ATT_GROUP):
            h = g * ATT_GROUP + hh
            o_ref[0, :, h * ATT_HEAD_DIM:(h + 1) * ATT_HEAD_DIM] = (
                o[hh * tq:(hh + 1) * tq]).astype(o_ref.dtype)


def dsa_mixer(q, k, v, iq, small, *, bsz, s_len, tq=128, ck=512):
    topk = min(TOPK_MAX, s_len // 4)
    ck = min(ck, s_len)
    assert ck >= topk and s_len % ck == 0 and s_len % tq == 0
    nq = s_len // tq
    hq = ATT_N_HEADS * ATT_HEAD_DIM
    hk = ATT_N_KV * ATT_HEAD_DIM
    hi = IDX_N_HEADS * IDX_HEAD_DIM
    r3 = lambda a: a.reshape(bsz, s_len, a.shape[-1])
    out = pl.pallas_call(
        functools.partial(_dsa_kernel, tq=tq, ck=ck, topk=topk),
        grid=(bsz, nq),
        in_specs=[pl.BlockSpec((1, tq, hq), lambda b, i: (b, i, 0)),
                  pl.BlockSpec((1, tq, hi), lambda b, i: (b, i, 0)),
                  pl.BlockSpec((1, tq, LANES), lambda b, i: (b, i, 0)),
                  pl.BlockSpec((1, s_len, hk), lambda b, i: (b, 0, 0)),
                  pl.BlockSpec((1, s_len, hk), lambda b, i: (b, 0, 0)),
                  pl.BlockSpec((1, s_len, LANES), lambda b, i: (b, 0, 0))],
        out_specs=pl.BlockSpec((1, tq, hq), lambda b, i: (b, i, 0)),
        out_shape=jax.ShapeDtypeStruct((bsz, s_len, hq), BF16),
        scratch_shapes=[pltpu.VMEM((s_len // ck, tq, ck), I32),
                        pltpu.VMEM((2, s_len // ck, ATT_GROUP * tq, ck), F32),
                        pltpu.VMEM((2, ATT_GROUP * tq, LANES), F32),
                        pltpu.VMEM((ATT_GROUP * tq, LANES), F32),
                        pltpu.VMEM((ATT_GROUP * tq, ATT_HEAD_DIM), F32)],
        compiler_params=_cparams(("parallel", "arbitrary")),
        name="dsa_mixer",
    )(r3(q), r3(iq), r3(small), r3(k), r3(v), r3(small))
    return out.reshape(bsz * s_len, hq)


def _memattn_kernel(q_ref, kv_ref, o_ref):
    scale = MEM_HEAD_DIM ** -0.5
    for h in range(MEM_HEADS):
        sl = slice(h * MEM_HEAD_DIM, (h + 1) * MEM_HEAD_DIM)
        qh = q_ref[:, sl]
        kh = kv_ref[:, sl]
        vh = kv_ref[:, D_MODEL + h * MEM_HEAD_DIM:D_MODEL + (h + 1) * MEM_HEAD_DIM]
        s = lax.dot_general(qh, kh, _NT, preferred_element_type=F32) * scale
        p = jnp.exp(s - jnp.max(s, axis=1, keepdims=True))
        o = jnp.dot(p.astype(BF16), vh, preferred_element_type=F32)
        o_ref[:, sl] = (o / jnp.sum(p, axis=1, keepdims=True)).astype(o_ref.dtype)


def mem_attention(qm, kv, *, bsz, s_len, tq=512):
    m = bsz * s_len
    tq = min(tq, s_len)
    nq = s_len // tq
    n_mem = kv.shape[0] // bsz
    return pl.pallas_call(
        _memattn_kernel,
        grid=(bsz, nq),
        in_specs=[pl.BlockSpec((tq, D_MODEL), lambda b, i: (b * nq + i, 0)),
                  pl.BlockSpec((n_mem, 2 * D_MODEL), lambda b, i: (b, 0))],
        out_specs=pl.BlockSpec((tq, D_MODEL), lambda b, i: (b * nq + i, 0)),
        out_shape=jax.ShapeDtypeStruct((m, D_MODEL), BF16),
        compiler_params=_cparams(("parallel", "parallel")),
        name="mem_attention",
    )(qm, kv)


def _rope_tables(positions):
    m = positions.size
    pos = positions.astype(F32).reshape(m, 1)

    def unit(head_dim):
        rot = head_dim // ROPE_FRACTION
        half = rot // 2
        inv_freq = jnp.power(ROPE_THETA, -jnp.arange(half, dtype=F32) * 2.0 / rot)
        ang = pos * inv_freq
        cos, sin = jnp.cos(ang), jnp.sin(ang)
        one = jnp.ones((m, head_dim - rot), F32)
        zero = jnp.zeros((m, head_dim - rot), F32)
        zh = jnp.zeros((m, half), F32)
        c = jnp.concatenate([cos, cos, one], axis=1)
        s1 = jnp.concatenate([zh, sin, zero], axis=1)
        s2 = jnp.concatenate([-sin, zh, zero], axis=1)
        return c, s1, s2

    att = unit(ATT_HEAD_DIM)
    ic, is1, is2 = unit(IDX_HEAD_DIM)
    idx = tuple(jnp.concatenate([t, t], axis=1) for t in (ic, is1, is2))
    pad1 = jnp.ones((m, LANES - IDX_HEAD_DIM), F32)
    pad0 = jnp.zeros((m, LANES - IDX_HEAD_DIM), F32)
    small = (jnp.concatenate([ic, pad1], axis=1),
             jnp.concatenate([is1, pad0], axis=1),
             jnp.concatenate([is2, pad0], axis=1))
    return att, idx, small


def _prep_in_proj(w):
    offs = [0]
    for s in IN_PROJ_SIZES:
        offs.append(offs[-1] + s)
    seg = lambda i: w[:, offs[i]:offs[i + 1]]
    z, xbc, dt, q, k, v, iq, ik, iw, g_ssd, g_att = (seg(i) for i in range(11))
    d = w.shape[0]
    g = SSD_N_GROUPS
    xs_w = xbc[:, :SSD_D_INNER].reshape(d, g, SSD_GW)
    b_w = xbc[:, SSD_D_INNER:SSD_D_INNER + g * SSD_D_STATE].reshape(d, g, SSD_D_STATE)
    c_w = xbc[:, SSD_D_INNER + g * SSD_D_STATE:].reshape(d, g, SSD_D_STATE)
    xbc_gm = jnp.concatenate([xs_w, b_w, c_w], axis=2).reshape(d, g * SSD_XBC_GW)
    small = jnp.concatenate(
        [ik, iw, jnp.zeros((d, LANES - IDX_HEAD_DIM - IDX_N_HEADS), w.dtype),
         dt, jnp.zeros((d, LANES - SSD_N_HEADS), w.dtype)], axis=1)
    c16 = lambda a: a.astype(BF16)
    return dict(z=c16(z), xbc=c16(xbc_gm),
                q=c16(q), k=c16(k), v=c16(v), iq=c16(iq), small=c16(small),
                gates=c16(jnp.concatenate([g_ssd, g_att], axis=1)))


def _prep_ssd_params(conv_w, conv_b, dt_bias, a_log, d_skip, norm_g):
    g = SSD_N_GROUPS

    def gm_channels(a):
        r = a.shape[0]
        xs = a[:, :SSD_D_INNER].reshape(r, g, SSD_GW)
        b = a[:, SSD_D_INNER:SSD_D_INNER + g * SSD_D_STATE].reshape(r, g, SSD_D_STATE)
        c = a[:, SSD_D_INNER + g * SSD_D_STATE:].reshape(r, g, SSD_D_STATE)
        return jnp.transpose(jnp.concatenate([xs, b, c], axis=2), (1, 0, 2))

    def gm_heads(a):
        return jnp.pad(a.reshape(1, SSD_N_HEADS), ((0, 0), (0, LANES - SSD_N_HEADS)))

    return (gm_channels(conv_w), gm_channels(conv_b[None, :]), gm_heads(dt_bias),
            gm_heads(a_log), gm_heads(d_skip), norm_g.reshape(g, 1, SSD_GW))


def kernel(x, mem, positions, norm_mix_pre, norm_mix_post, w_in, ssd_conv_w, ssd_conv_b,
           ssd_dt_bias, ssd_a_log, ssd_d, ssd_norm, w_br_ssd, w_br_att, w_out,
           norm_mem_pre, norm_mem_post, mem_norm, w_mem_q, w_mem_kv, w_mem_o,
           norm_ffn_pre, norm_ffn_post, w_ffn_in, w_ffn_out):
    bsz, s_len, d = x.shape
    m = bsz * s_len
    depth = w_in.shape[0]
    x2 = x.reshape(m, d)
    mem2 = mem.reshape(bsz * mem.shape[1], d)
    rope_att, rope_idx, rope_small = _rope_tables(positions)
    rope_ex = lambda t: tuple((a, "m128") for a in t)
    tm = 1024
    gcol = D_MODEL // 1024

    gvec = lambda a: a.reshape(1, d)
    hn = rmsnorm_bf16(x2, norm_mix_pre[0])
    for l in range(depth):
        wp = _prep_in_proj(w_in[l])
        z_g = matmul(hn, wp["z"], tm=tm, tn=SSD_GW, out_gm=True, name="proj_z")
        xbc_g = matmul(hn, wp["xbc"], tm=tm, tn=SSD_XBC_GW, out_gm=True, name="proj_xbc")
        q = matmul(hn, wp["q"], tm=tm, tn=1024, out_dtype=BF16, name="proj_q",
                   epi=functools.partial(_epi_rope, ATT_HEAD_DIM // ROPE_FRACTION // 2, 1024 // LANES),
                   extras=rope_ex(rope_att))
        k = matmul(hn, wp["k"], tm=tm, tn=512, out_dtype=BF16, name="proj_k",
                   epi=functools.partial(_epi_rope, ATT_HEAD_DIM // ROPE_FRACTION // 2, 512 // LANES),
                   extras=rope_ex(rope_att))
        v = matmul(hn, wp["v"], tm=tm, tn=512, out_dtype=BF16, name="proj_v")
        iq = matmul(hn, wp["iq"], tm=tm, tn=1024, out_dtype=BF16, name="proj_iq",
                    epi=functools.partial(_epi_rope, IDX_HEAD_DIM // ROPE_FRACTION // 2, 1024 // LANES),
                    extras=rope_ex(rope_idx))
        small = matmul(hn, wp["small"], tm=tm, tn=2 * LANES, name="proj_small",
                       epi=functools.partial(_epi_rope, IDX_HEAD_DIM // ROPE_FRACTION // 2, 1),
                       extras=rope_ex(rope_small))
        gates = matmul(hn, wp["gates"], tm=tm, tn=1024, epi=_epi_sigmoid, name="proj_gates")

        ssd_par = _prep_ssd_params(ssd_conv_w[l], ssd_conv_b[l], ssd_dt_bias[l],
                                   ssd_a_log[l], ssd_d[l], ssd_norm[l])
        y_g = ssd_mixer(z_g, xbc_g, small, *ssd_par, bsz=bsz, s_len=s_len)
        att = dsa_mixer(q, k, v, iq, small, bsz=bsz, s_len=s_len)

        u_ssd = matmul(y_g, w_br_ssd[l].astype(BF16), tm=tm, tn=1024, a_gm=True,
                       epi=_epi_gate_mul, extras=((gates, "mn", 0),), name="branch_ssd")
        merged = matmul(att, w_br_att[l].astype(BF16), tm=tm, tn=1024, out_dtype=BF16,
                        epi=_epi_gate_mul_add, extras=((gates, "mn", gcol), (u_ssd, "mn", 0)),
                        name="branch_att_merge")
        x2, hn = matmul(merged, w_out[l].astype(BF16), tm=256, tn=D_MODEL, epi=_epi_norm_res_next,
                        extras=((gvec(norm_mix_post[l]), "n"), (x2, "mn"), (gvec(norm_mem_pre[l]), "n")),
                        out2_dtype=BF16, name="mix_out")

        qm = matmul(hn, w_mem_q[l].astype(BF16), tm=tm, tn=1024, out_dtype=BF16, name="mem_q")
        mem_n = rmsnorm_bf16(mem2, mem_norm[l])
        kvm = matmul(mem_n, w_mem_kv[l].astype(BF16), tm=tm, tn=1024, out_dtype=BF16, name="mem_kv")
        om = mem_attention(qm, kvm, bsz=bsz, s_len=s_len)
        x2, hn = matmul(om, w_mem_o[l].astype(BF16), tm=256, tn=D_MODEL, epi=_epi_norm_res_next,
                        extras=((gvec(norm_mem_post[l]), "n"), (x2, "mn"), (gvec(norm_ffn_pre[l]), "n")),
                        out2_dtype=BF16, name="mem_out")

        wf = w_ffn_in[l]
        hf = swiglu_in(hn, wf[:, :D_FF].astype(BF16), wf[:, D_FF:].astype(BF16), tm=tm, tn=512)
        w_fo = w_ffn_out[l].astype(BF16)
        if l + 1 < depth:
            x2, hn = matmul(hf, w_fo, tm=512, tn=D_MODEL, tk=D_FF // 4, epi=_epi_norm_res_next,
                            extras=((gvec(norm_ffn_post[l]), "n"), (x2, "mn"),
                                    (gvec(norm_mix_pre[l + 1]), "n")),
                            out2_dtype=BF16, name="ffn_out")
        else:
            x2 = matmul(hf, w_fo, tm=512, tn=D_MODEL, tk=D_FF // 4, epi=_epi_norm_res,
                        extras=((gvec(norm_ffn_post[l]), "n"), (x2, "mn")), name="ffn_out_last")

    return x2.reshape(bsz, s_len, d)
```

```python
import functools
import math

import jax
import jax.numpy as jnp
from jax import lax
from jax.experimental import pallas as pl
from jax.experimental.pallas import tpu as pltpu

F32 = jnp.float32
BF16 = jnp.bfloat16
I32 = jnp.int32

D_MODEL = 2048
N_MEM = 256
SSD_D_INNER = 4096
SSD_HEAD_DIM = 64
SSD_N_HEADS = 64
SSD_N_GROUPS = 8
SSD_HPG = SSD_N_HEADS // SSD_N_GROUPS
SSD_GW = SSD_D_INNER // SSD_N_GROUPS
SSD_D_STATE = 128
SSD_CONV = 4
SSD_CHUNK = 128
SSD_XBC_GW = SSD_GW + 2 * SSD_D_STATE
ATT_HEAD_DIM = 128
ATT_N_HEADS = 16
ATT_N_KV = 4
ATT_GROUP = ATT_N_HEADS // ATT_N_KV
IDX_N_HEADS = 16
IDX_HEAD_DIM = 64
TOPK_MAX = 256
ROPE_THETA = 500000.0
ROPE_FRACTION = 4
MEM_HEADS = 4
MEM_HEAD_DIM = D_MODEL // MEM_HEADS
D_FF = 5632
NORM_EPS = 1e-6
IN_PROJ_SIZES = (4096, 6144, 64, 2048, 512, 512, 1024, 64, 16, 2048, 2048)

LANES = 128
SUBLANES = 8
V7X_VMEM_BYTES = 64 * 1024 * 1024
VMEM_LIMIT = V7X_VMEM_BYTES - 8 * 1024 * 1024

INT_MIN = -(2 ** 31)
NEG_BIG = -1e30

_NT = (((1,), (1,)), ((), ()))


def _cparams(sem):
    return pltpu.CompilerParams(dimension_semantics=sem, vmem_limit_bytes=VMEM_LIMIT)


def _rmsnorm_kernel(x_ref, g_ref, o_ref):
    x = x_ref[...]
    ms = jnp.mean(x * x, axis=-1, keepdims=True)
    o_ref[...] = (x * lax.rsqrt(ms + NORM_EPS) * g_ref[...]).astype(o_ref.dtype)


def rmsnorm_bf16(x2, g, tm=256):
    m, d = x2.shape
    tm = min(tm, m)
    return pl.pallas_call(
        _rmsnorm_kernel,
        grid=(m // tm,),
        in_specs=[pl.BlockSpec((tm, d), lambda i: (i, 0)),
                  pl.BlockSpec((1, d), lambda i: (0, 0))],
        out_specs=pl.BlockSpec((tm, d), lambda i: (i, 0)),
        out_shape=jax.ShapeDtypeStruct((m, d), BF16),
        compiler_params=_cparams(("parallel",)),
        name="rmsnorm",
    )(x2, g.reshape(1, d))


def _epi_none(acc):
    return acc


def _epi_sigmoid(acc):
    return jax.nn.sigmoid(acc)


def _epi_silu(acc):
    return acc * jax.nn.sigmoid(acc)


def _epi_gate_mul(acc, gate_ref):
    return acc * gate_ref[...].astype(F32)


def _epi_gate_mul_add(acc, gate_ref, add_ref):
    return acc * gate_ref[...].astype(F32) + add_ref[...].astype(F32)


def _epi_rope(shift, n_rot, acc, c_ref, s1_ref, s2_ref):
    c = c_ref[...]
    s1 = s1_ref[...]
    s2 = s2_ref[...]
    outs = []
    for j in range(acc.shape[1] // LANES):
        sl = acc[:, j * LANES:(j + 1) * LANES]
        if j < n_rot:
            sl = (sl * c + pltpu.roll(sl, shift, 1) * s1
                  + pltpu.roll(sl, LANES - shift, 1) * s2)
        outs.append(sl)
    return outs[0] if len(outs) == 1 else jnp.concatenate(outs, axis=1)


def _epi_norm_res(acc, g_ref, x_ref):
    ms = jnp.mean(acc * acc, axis=-1, keepdims=True)
    return x_ref[...] + acc * lax.rsqrt(ms + NORM_EPS) * g_ref[...]


def _epi_norm_res_next(acc, g_ref, x_ref, gnext_ref):
    xn = _epi_norm_res(acc, g_ref, x_ref)
    ms = jnp.mean(xn * xn, axis=-1, keepdims=True)
    return xn, xn * lax.rsqrt(ms + NORM_EPS) * gnext_ref[...]


def _mm_kernel(epi, n_extra, n_out, nk, a_ref, w_ref, *refs):
    extras = refs[:n_extra]
    o_refs = refs[n_extra:n_extra + n_out]

    def finish(acc):
        vals = epi(acc, *extras)
        vals = vals if isinstance(vals, tuple) else (vals,)
        for o_ref, val in zip(o_refs, vals, strict=True):
            o_ref[...] = val.astype(o_ref.dtype).reshape(o_ref.shape)

    if len(a_ref.shape) == 3 and a_ref.shape[0] > 1:
        ng, _, gk = a_ref.shape
        part = jnp.dot(a_ref[0], w_ref[0:gk, :], preferred_element_type=F32)
        for gi in range(1, ng):
            part = part + jnp.dot(a_ref[gi], w_ref[gi * gk:(gi + 1) * gk, :],
                                  preferred_element_type=F32)
        finish(part)
        return
    a = a_ref[...]
    a = a.reshape(a.shape[-2], a.shape[-1])
    part = jnp.dot(a, w_ref[...], preferred_element_type=F32)
    if nk == 1:
        finish(part)
        return
    acc_ref = refs[n_extra + n_out]
    k = pl.program_id(2)

    @pl.when(k == 0)
    def _():
        acc_ref[...] = part

    @pl.when(k > 0)
    def _():
        acc_ref[...] += part

    @pl.when(k == nk - 1)
    def _():
        finish(acc_ref[...])


def matmul(a, w, *, tm, tn, tk=None, out_dtype=F32, epi=_epi_none, extras=(),
           a_gm=False, out_gm=False, out2_dtype=None, name="matmul"):
    kdim, n = w.shape
    if a_gm == "full":
        ng, m, gk = a.shape
        assert ng * gk == kdim
        nk, tk = 1, kdim
    elif a_gm:
        nk, m, tk = a.shape
        assert nk * tk == kdim
    else:
        m = a.shape[0]
        tk = kdim if tk is None else tk
        assert kdim % tk == 0
        nk = kdim // tk
    tm = min(tm, m)
    assert m % tm == 0 and n % tn == 0
    if a_gm == "full":
        a_spec = pl.BlockSpec((ng, tm, gk), lambda i, j, k: (0, i, 0))
    elif a_gm:
        a_spec = pl.BlockSpec((1, tm, tk), lambda i, j, k: (k, i, 0))
    else:
        a_spec = pl.BlockSpec((tm, tk), lambda i, j, k: (i, k))
    in_specs = [a_spec, pl.BlockSpec((tk, tn), lambda i, j, k: (k, j))]
    args = [a, w]
    for ex in extras:
        arr, kind = ex[0], ex[1]
        off = ex[2] if len(ex) > 2 else 0
        if kind == "mn":
            in_specs.append(pl.BlockSpec((tm, tn), lambda i, j, k, off=off: (i, j + off)))
        elif kind == "m128":
            in_specs.append(pl.BlockSpec((tm, LANES), lambda i, j, k: (i, 0)))
        elif kind == "n":
            in_specs.append(pl.BlockSpec((1, tn), lambda i, j, k: (0, j)))
        else:
            raise ValueError(kind)
        args.append(arr)
    if out_gm:
        out_spec = pl.BlockSpec((1, tm, tn), lambda i, j, k: (j, i, 0))
        out_shape = jax.ShapeDtypeStruct((n // tn, m, tn), out_dtype)
    else:
        out_spec = pl.BlockSpec((tm, tn), lambda i, j, k: (i, j))
        out_shape = jax.ShapeDtypeStruct((m, n), out_dtype)
    n_out = 1
    if out2_dtype is not None:
        assert not out_gm
        n_out = 2
        out_spec = [out_spec, pl.BlockSpec((tm, tn), lambda i, j, k: (i, j))]
        out_shape = [out_shape, jax.ShapeDtypeStruct((m, n), out2_dtype)]
    scratch = [pltpu.VMEM((tm, tn), F32)] if nk > 1 else []
    return pl.pallas_call(
        functools.partial(_mm_kernel, epi, len(extras), n_out, nk),
        grid=(m // tm, n // tn, nk),
        in_specs=in_specs,
        out_specs=out_spec,
        out_shape=out_shape,
        scratch_shapes=scratch,
        compiler_params=_cparams(("parallel", "parallel", "arbitrary")),
        name=name,
    )(*args)


def _swiglu_kernel(a_ref, wg_ref, wu_ref, o_ref):
    a = a_ref[...]
    g = jnp.dot(a, wg_ref[...], preferred_element_type=F32)
    u = jnp.dot(a, wu_ref[...], preferred_element_type=F32)
    o_ref[...] = (g * jax.nn.sigmoid(g) * u).astype(o_ref.dtype)


def swiglu_in(a, wg, wu, *, tm, tn):
    m, kdim = a.shape
    n = wg.shape[1]
    tm = min(tm, m)
    return pl.pallas_call(
        _swiglu_kernel,
        grid=(m // tm, n // tn),
        in_specs=[pl.BlockSpec((tm, kdim), lambda i, j: (i, 0)),
                  pl.BlockSpec((kdim, tn), lambda i, j: (0, j)),
                  pl.BlockSpec((kdim, tn), lambda i, j: (0, j))],
        out_specs=pl.BlockSpec((tm, tn), lambda i, j: (i, j)),
        out_shape=jax.ShapeDtypeStruct((m, n), BF16),
        compiler_params=_cparams(("parallel", "parallel")),
        name="swiglu_in",
    )(a, wg, wu)


def _xbc_conv_kernel(a_ref, w_ref, cw_ref, cb_ref, o_ref, ubuf, *, tiles_per_seq):
    i = pl.program_id(1)
    tm = a_ref.shape[0]
    acc = jnp.dot(a_ref[...], w_ref[...], preferred_element_type=F32)
    prev_tail = ubuf[tm:tm + SUBLANES, :]
    seq_start = lax.rem(i, tiles_per_seq) == 0
    ubuf[0:SUBLANES, :] = jnp.where(seq_start, 0.0, prev_tail)
    ubuf[SUBLANES:SUBLANES + tm, :] = acc
    w = cw_ref[0]
    conv = cb_ref[0]
    for k in range(SSD_CONV):
        conv = conv + ubuf[pl.ds(SUBLANES - (SSD_CONV - 1) + k, tm), :] * w[k:k + 1, :]
    o_ref[0] = (conv * jax.nn.sigmoid(conv)).astype(o_ref.dtype)


def xbc_conv_proj(a, w, cw_g, cb_g, *, s_len, tm):
    m, kdim = a.shape
    g, gw = SSD_N_GROUPS, SSD_XBC_GW
    tm = min(tm, s_len)
    assert s_len % tm == 0 and m % s_len == 0
    return pl.pallas_call(
        functools.partial(_xbc_conv_kernel, tiles_per_seq=s_len // tm),
        grid=(g, m // tm),
        in_specs=[pl.BlockSpec((tm, kdim), lambda j, i: (i, 0)),
                  pl.BlockSpec((kdim, gw), lambda j, i: (0, j)),
                  pl.BlockSpec((1, SSD_CONV, gw), lambda j, i: (j, 0, 0)),
                  pl.BlockSpec((1, 1, gw), lambda j, i: (j, 0, 0))],
        out_specs=pl.BlockSpec((1, tm, gw), lambda j, i: (j, i, 0)),
        out_shape=jax.ShapeDtypeStruct((g, m, gw), F32),
        scratch_shapes=[pltpu.VMEM((tm + SUBLANES, gw), F32)],
        compiler_params=_cparams(("arbitrary", "arbitrary")),
        name="proj_xbc_conv",
    )(a, w, cw_g, cb_g)


def _split3_bf16(x):
    h1 = x.astype(BF16)
    r1 = x - h1.astype(F32)
    h2 = r1.astype(BF16)
    h3 = (r1 - h2.astype(F32)).astype(BF16)
    return h1, h2, h3


def _split2_bf16(x):
    hi = x.astype(BF16)
    return hi, (x - hi.astype(F32)).astype(BF16)


def _ssd_kernel(zs_ref, xbc_ref, dt_ref, dtb_ref, alog_ref, d_ref, ng_ref, ex_ref,
                y_ref, st_scr, cst_scr):
    q = SSD_CHUNK
    c_idx = pl.program_id(1)

    @pl.when(c_idx == 0)
    def _():
        st_scr[...] = jnp.zeros(st_scr.shape, F32)

    row = lax.broadcasted_iota(I32, (q, q), 0)
    col = lax.broadcasted_iota(I32, (q, q), 1)
    causal = row >= col
    tril16 = causal.astype(F32).astype(BF16)
    lo = lax.broadcasted_iota(I32, (q, LANES), 1) < SSD_HEAD_DIM

    dtr = dt_ref[...] + dtb_ref[...]
    dt_all = jnp.maximum(dtr, 0.0) + jnp.log1p(jnp.exp(-jnp.abs(dtr)))
    da = dt_all * (-jnp.exp(alog_ref[...]))
    h1, h2, h3 = _split3_bf16(da)
    cs_all = (jnp.dot(tril16, h1, preferred_element_type=F32)
              + jnp.dot(tril16, h2, preferred_element_type=F32)
              + jnp.dot(tril16, h3, preferred_element_type=F32))
    cst_scr[...] = cs_all.T
    last_all = cs_all[q - 1:q, :]
    ecs_all = jnp.exp(cs_all)
    wend_all = jnp.exp(last_all - cs_all)
    elast_all = jnp.exp(last_all)
    tok_splits = [_split2_bf16(v) for v in (dt_all, ecs_all, wend_all)]
    row_vals = jnp.concatenate(
        [elast_all, d_ref[...], jnp.zeros((SUBLANES - 2, LANES), F32)], axis=0)
    row_splits = _split3_bf16(row_vals)

    def group(g, carry):
        ex = ex_ref[g]
        spread = lambda parts: sum(jnp.dot(h, ex, preferred_element_type=F32) for h in parts)
        dt_x, ecs_x, wend_x = (spread(parts) for parts in tok_splits)
        rows_x = spread(row_splits)
        el = rows_x[0:1, :]
        dsk_x = rows_x[1:2, :]
        cs = pltpu.roll(cs_all, lax.rem(LANES - SSD_HPG * g, LANES), 1)
        cs_t = cst_scr[pl.ds(pl.multiple_of(g * SSD_HPG, SSD_HPG), SSD_HPG), :]

        xs = xbc_ref[g, :, 0:SSD_GW]
        bm = xbc_ref[g, :, SSD_GW:SSD_GW + SSD_D_STATE]
        cm = xbc_ref[g, :, SSD_GW + SSD_D_STATE:SSD_XBC_GW]

        bm16 = bm.astype(BF16)
        cm16 = cm.astype(BF16)
        cb = lax.dot_general(cm16, bm16, _NT, preferred_element_type=F32)
        st_t = st_scr[g]
        ystate = jnp.dot(cm16, st_t.astype(BF16), preferred_element_type=F32)

        xdt_g = xs * dt_x
        xw = (xdt_g * wend_x).astype(BF16)
        ys = []
        for p in range(SSD_HPG // 2):
            j0 = 2 * p
            xdt = xdt_g[:, p * LANES:(p + 1) * LANES]
            lmats = []
            for j in (j0, j0 + 1):
                diff = cs[:, j:j + 1] - cs_t[j:j + 1, :]
                dec = jnp.exp(jnp.where(causal, diff, -jnp.inf))
                lmats.append((cb * dec).astype(BF16))
            lmat = jnp.concatenate(lmats, axis=1)
            rhs = jnp.concatenate([jnp.where(lo, xdt, 0.0), jnp.where(lo, 0.0, xdt)],
                                  axis=0).astype(BF16)
            ys.append(jnp.dot(lmat, rhs, preferred_element_type=F32))
        y = jnp.concatenate(ys, axis=1) + ystate * ecs_x + dsk_x * xs
        st_scr[g] = st_t * el + jnp.dot(bm.T.astype(BF16), xw, preferred_element_type=F32)

        y = y * zs_ref[g]
        y = y * lax.rsqrt(jnp.mean(y * y, axis=-1, keepdims=True) + NORM_EPS) * ng_ref[g]
        y_ref[g] = y.astype(y_ref.dtype)
        return carry

    lax.fori_loop(0, SSD_N_GROUPS, group, 0)


def _head_spread_table():
    head = lax.broadcasted_iota(I32, (SSD_N_GROUPS, LANES, SSD_GW), 1)
    grp = lax.broadcasted_iota(I32, (SSD_N_GROUPS, LANES, SSD_GW), 0)
    chan = lax.broadcasted_iota(I32, (SSD_N_GROUPS, LANES, SSD_GW), 2)
    return (head == grp * SSD_HPG + chan // SSD_HEAD_DIM).astype(BF16)


def ssd_mixer(zs_g, xbc_g, small, dtb, alog, dskip, ng_g, *, bsz, s_len):
    g, q = SSD_N_GROUPS, SSD_CHUNK
    m = bsz * s_len
    nc = s_len // q

    def act(w):
        return pl.BlockSpec((g, q, w), lambda b, c: (0, b * nc + c, 0))

    def par(r, w):
        return pl.BlockSpec((g, r, w), lambda b, c: (0, 0, 0))

    head = pl.BlockSpec((1, LANES), lambda b, c: (0, 0))

    return pl.pallas_call(
        _ssd_kernel,
        grid=(bsz, nc),
        in_specs=[act(SSD_GW), act(SSD_XBC_GW),
                  pl.BlockSpec((q, LANES), lambda b, c: (b * nc + c, 1)),
                  head, head, head, par(1, SSD_GW), par(LANES, SSD_GW)],
        out_specs=act(SSD_GW),
        out_shape=jax.ShapeDtypeStruct((g, m, SSD_GW), BF16),
        scratch_shapes=[pltpu.VMEM((g, SSD_D_STATE, SSD_GW), F32),
                        pltpu.VMEM((LANES, q), F32)],
        compiler_params=_cparams(("arbitrary", "arbitrary")),
        name="ssd_mixer",
    )(zs_g, xbc_g, small, dtb, alog, dskip, ng_g, _head_spread_table())


def _dsa_kernel(q_ref, iq_ref, smq_ref, k_ref, v_ref, smk_ref, o_ref,
                key_scr, s_scr, mx_scr, l_scr, acc_scr, *, tq, ck, topk):
    i = pl.program_id(1)
    q0 = i * tq
    nch = lax.div(q0 + tq + ck - 1, ck)
    idx_scale = (IDX_HEAD_DIM ** -0.5) * (IDX_N_HEADS ** -0.5)
    att_scale = ATT_HEAD_DIM ** -0.5

    iw = smq_ref[0][:, IDX_HEAD_DIM:IDX_HEAD_DIM + IDX_N_HEADS]
    tpos = q0 + lax.broadcasted_iota(I32, (tq, ck), 0)
    lane_pos = lax.broadcasted_iota(I32, (tq, ck), 1)

    def score_chunk(c, carry):
        s0 = pl.multiple_of(c * ck, ck)
        ikc = smk_ref[0, pl.ds(s0, ck), :][:, :IDX_HEAD_DIM].astype(BF16)
        acc = jnp.zeros((tq, ck), F32)
        for h in range(IDX_N_HEADS):
            iqh = iq_ref[0, :, h * IDX_HEAD_DIM:(h + 1) * IDX_HEAD_DIM]
            d = lax.dot_general(iqh, ikc, _NT, preferred_element_type=F32)
            acc = acc + jnp.maximum(d, 0.0) * iw[:, h:h + 1]
        acc = acc * idx_scale
        bits = pltpu.bitcast(acc, I32)
        key = jnp.where(bits < 0, bits ^ jnp.int32(0x7FFFFFFF), bits)
        key_scr[c] = jnp.where(s0 + lane_pos <= tpos, key, jnp.int32(INT_MIN))
        return carry

    lax.fori_loop(0, nch, score_chunk, 0)

    def count_ge(cand):
        cand_b = jnp.broadcast_to(cand, (tq, LANES))

        def body(c, cnt):
            kc = key_scr[c]
            for u in range(ck // LANES):
                cnt = cnt + (kc[:, u * LANES:(u + 1) * LANES] >= cand_b).astype(I32)
            return cnt

        cnt = lax.fori_loop(0, nch, body, jnp.zeros((tq, LANES), I32))
        return jnp.sum(cnt.astype(F32), axis=1, keepdims=True)

    keep_all = q0 + lax.broadcasted_iota(I32, (tq, 1), 0) < topk

    def bis_cond(st):
        it, _, _, pending = st
        return jnp.logical_and(it < 32, pending > 0.0)

    def bis_step(st):
        it, thr, cnt_thr, _ = st
        cand = thr + lax.shift_left(jnp.int32(1), 31 - it)
        cnt = count_ge(cand)
        take = cnt >= float(topk)
        thr = jnp.where(take, cand, thr)
        cnt_thr = jnp.where(take, cnt, cnt_thr)
        done = jnp.logical_or(cnt_thr == float(topk), keep_all)
        return it + 1, thr, cnt_thr, jnp.sum(jnp.where(done, 0.0, 1.0))

    _, thr, _, _ = lax.while_loop(
        bis_cond, bis_step,
        (jnp.int32(0), jnp.full((tq, 1), INT_MIN, I32), jnp.full((tq, 1), 2.0 * topk, F32),
         jnp.float32(1.0)))
    thr = jnp.where(keep_all, jnp.int32(INT_MIN + 1), jnp.maximum(thr, jnp.int32(INT_MIN + 1)))
    thr_b = jnp.broadcast_to(thr, (tq, LANES))

    rows = ATT_GROUP * tq
    exp2_scale = att_scale * math.log2(math.e)

    def stacked_q(g):
        return jnp.concatenate(
            [q_ref[0, :, (g * ATT_GROUP + hh) * ATT_HEAD_DIM:(g * ATT_GROUP + hh + 1) * ATT_HEAD_DIM]
             for hh in range(ATT_GROUP)], axis=0)

    def logits_pass(g, qg, chunks):
        ksl = slice(g * ATT_HEAD_DIM, (g + 1) * ATT_HEAD_DIM)
        mxs = [mx_scr[g % 2, hh * tq:(hh + 1) * tq, :] for hh in range(ATT_GROUP)]
        for c in chunks:
            s0 = pl.multiple_of(c * ck, ck)
            s = lax.dot_general(qg, k_ref[0, pl.ds(s0, ck), ksl], _NT,
                                preferred_element_type=F32)
            kc = key_scr[c]
            sel = [kc[:, u * LANES:(u + 1) * LANES] >= thr_b for u in range(ck // LANES)]
            for hh in range(ATT_GROUP):
                rsl = slice(hh * tq, (hh + 1) * tq)
                for u in range(ck // LANES):
                    lsl = slice(u * LANES, (u + 1) * LANES)
                    sm = jnp.where(sel[u], s[rsl, lsl], NEG_BIG)
                    s_scr[g % 2, c, rsl, lsl] = sm
                    mxs[hh] = jnp.maximum(mxs[hh], sm)
        for hh in range(ATT_GROUP):
            mx_scr[g % 2, hh * tq:(hh + 1) * tq, :] = mxs[hh]

    def pv_pass(g, m_b, chunks):
        ksl = slice(g * ATT_HEAD_DIM, (g + 1) * ATT_HEAD_DIM)
        lsum = l_scr[...]
        acc = acc_scr[...]
        for c in chunks:
            s0 = pl.multiple_of(c * ck, ck)
            ps = []
            for u in range(ck // LANES):
                p = jnp.exp2((s_scr[g % 2, c, :, u * LANES:(u + 1) * LANES] - m_b) * exp2_scale)
                lsum = lsum + p
                ps.append(p.astype(BF16))
            acc = acc + jnp.dot(jnp.concatenate(ps, axis=1), v_ref[0, pl.ds(s0, ck), ksl],
                                preferred_element_type=F32)
        l_scr[...] = lsum
        acc_scr[...] = acc

    def chunk_loop(body):
        def pair(t, carry):
            body((2 * t, 2 * t + 1))
            return carry

        lax.fori_loop(0, lax.shift_right_logical(nch, 1), pair, 0)

        @pl.when(lax.rem(nch, 2) == 1)
        def _():
            body((nch - 1,))

    mx_scr[...] = jnp.full(mx_scr.shape, NEG_BIG, F32)
    q_cur = stacked_q(0)
    chunk_loop(lambda chunks: logits_pass(0, q_cur, chunks))
    for g in range(ATT_N_KV):
        m_b = jnp.broadcast_to(jnp.max(mx_scr[g % 2], axis=1, keepdims=True), (rows, LANES))
        l_scr[...] = jnp.zeros(l_scr.shape, F32)
        acc_scr[...] = jnp.zeros(acc_scr.shape, F32)
        if g + 1 < ATT_N_KV:
            mx_scr[(g + 1) % 2] = jnp.full((rows, LANES), NEG_BIG, F32)
            q_nxt = stacked_q(g + 1)

            def fused(chunks, g=g, m_b=m_b, q_nxt=q_nxt):
                pv_pass(g, m_b, chunks)
                logits_pass(g + 1, q_nxt, chunks)
        else:
            def fused(chunks, g=g, m_b=m_b):
                pv_pass(g, m_b, chunks)

        chunk_loop(fused)
        o = acc_scr[...] / jnp.sum(l_scr[...], axis=1, keepdims=True)
        for hh in range(ATT_GROUP):
            h = g * ATT_GROUP + hh
            o_ref[0, :, h * ATT_HEAD_DIM:(h + 1) * ATT_HEAD_DIM] = (
                o[hh * tq:(hh + 1) * tq]).astype(o_ref.dtype)


def dsa_mixer(q, k, v, iq, small, *, bsz, s_len, tq=128, ck=512):
    topk = min(TOPK_MAX, s_len // 4)
    ck = min(ck, s_len)
    assert ck >= topk and s_len % ck == 0 and s_len % tq == 0
    nq = s_len // tq
    hq = ATT_N_HEADS * ATT_HEAD_DIM
    hk = ATT_N_KV * ATT_HEAD_DIM
    hi = IDX_N_HEADS * IDX_HEAD_DIM
    r3 = lambda a: a.reshape(bsz, s_len, a.shape[-1])
    out = pl.pallas_call(
        functools.partial(_dsa_kernel, tq=tq, ck=ck, topk=topk),
        grid=(bsz, nq),
        in_specs=[pl.BlockSpec((1, tq, hq), lambda b, i: (b, i, 0)),
                  pl.BlockSpec((1, tq, hi), lambda b, i: (b, i, 0)),
                  pl.BlockSpec((1, tq, LANES), lambda b, i: (b, i, 0)),
                  pl.BlockSpec((1, s_len, hk), lambda b, i: (b, 0, 0)),
                  pl.BlockSpec((1, s_len, hk), lambda b, i: (b, 0, 0)),
                  pl.BlockSpec((1, s_len, LANES), lambda b, i: (b, 0, 0))],
        out_specs=pl.BlockSpec((1, tq, hq), lambda b, i: (b, i, 0)),
        out_shape=jax.ShapeDtypeStruct((bsz, s_len, hq), BF16),
        scratch_shapes=[pltpu.VMEM((s_len // ck, tq, ck), I32),
                        pltpu.VMEM((2, s_len // ck, ATT_GROUP * tq, ck), F32),
                        pltpu.VMEM((2, ATT_GROUP * tq, LANES), F32),
                        pltpu.VMEM((ATT_GROUP * tq, LANES), F32),
                        pltpu.VMEM((ATT_GROUP * tq, ATT_HEAD_DIM), F32)],
        compiler_params=_cparams(("parallel", "arbitrary")),
        name="dsa_mixer",
    )(r3(q), r3(iq), r3(small), r3(k), r3(v), r3(small))
    return out.reshape(bsz * s_len, hq)


def _memattn_kernel(q_ref, kv_ref, o_ref):
    scale = MEM_HEAD_DIM ** -0.5
    for h in range(MEM_HEADS):
        sl = slice(h * MEM_HEAD_DIM, (h + 1) * MEM_HEAD_DIM)
        qh = q_ref[:, sl]
        kh = kv_ref[:, sl]
        vh = kv_ref[:, D_MODEL + h * MEM_HEAD_DIM:D_MODEL + (h + 1) * MEM_HEAD_DIM]
        s = lax.dot_general(qh, kh, _NT, preferred_element_type=F32) * scale
        p = jnp.exp(s - jnp.max(s, axis=1, keepdims=True))
        o = jnp.dot(p.astype(BF16), vh, preferred_element_type=F32)
        o_ref[:, sl] = (o / jnp.sum(p, axis=1, keepdims=True)).astype(o_ref.dtype)


def mem_attention(qm, kv, *, bsz, s_len, tq=512):
    m = bsz * s_len
    tq = min(tq, s_len)
    nq = s_len // tq
    n_mem = kv.shape[0] // bsz
    return pl.pallas_call(
        _memattn_kernel,
        grid=(bsz, nq),
        in_specs=[pl.BlockSpec((tq, D_MODEL), lambda b, i: (b * nq + i, 0)),
                  pl.BlockSpec((n_mem, 2 * D_MODEL), lambda b, i: (b, 0))],
        out_specs=pl.BlockSpec((tq, D_MODEL), lambda b, i: (b * nq + i, 0)),
        out_shape=jax.ShapeDtypeStruct((m, D_MODEL), BF16),
        compiler_params=_cparams(("parallel", "parallel")),
        name="mem_attention",
    )(qm, kv)


def _rope_tables(positions):
    m = positions.size
    pos = positions.astype(F32).reshape(m, 1)

    def unit(head_dim):
        rot = head_dim // ROPE_FRACTION
        half = rot // 2
        inv_freq = jnp.power(ROPE_THETA, -jnp.arange(half, dtype=F32) * 2.0 / rot)
        ang = pos * inv_freq
        cos, sin = jnp.cos(ang), jnp.sin(ang)
        one = jnp.ones((m, head_dim - rot), F32)
        zero = jnp.zeros((m, head_dim - rot), F32)
        zh = jnp.zeros((m, half), F32)
        c = jnp.concatenate([cos, cos, one], axis=1)
        s1 = jnp.concatenate([zh, sin, zero], axis=1)
        s2 = jnp.concatenate([-sin, zh, zero], axis=1)
        return c, s1, s2

    att = unit(ATT_HEAD_DIM)
    ic, is1, is2 = unit(IDX_HEAD_DIM)
    idx = tuple(jnp.concatenate([t, t], axis=1) for t in (ic, is1, is2))
    pad1 = jnp.ones((m, LANES - IDX_HEAD_DIM), F32)
    pad0 = jnp.zeros((m, LANES - IDX_HEAD_DIM), F32)
    small = (jnp.concatenate([ic, pad1], axis=1),
             jnp.concatenate([is1, pad0], axis=1),
             jnp.concatenate([is2, pad0], axis=1))
    return att, idx, small


def _prep_in_proj(w):
    offs = [0]
    for s in IN_PROJ_SIZES:
        offs.append(offs[-1] + s)
    seg = lambda i: w[:, offs[i]:offs[i + 1]]
    z, xbc, dt, q, k, v, iq, ik, iw, g_ssd, g_att = (seg(i) for i in range(11))
    d = w.shape[0]
    g = SSD_N_GROUPS
    xs_w = xbc[:, :SSD_D_INNER].reshape(d, g, SSD_GW)
    b_w = xbc[:, SSD_D_INNER:SSD_D_INNER + g * SSD_D_STATE].reshape(d, g, SSD_D_STATE)
    c_w = xbc[:, SSD_D_INNER + g * SSD_D_STATE:].reshape(d, g, SSD_D_STATE)
    xbc_gm = jnp.concatenate([xs_w, b_w, c_w], axis=2).reshape(d, g * SSD_XBC_GW)
    small = jnp.concatenate(
        [ik, iw, jnp.zeros((d, LANES - IDX_HEAD_DIM - IDX_N_HEADS), w.dtype),
         dt, jnp.zeros((d, LANES - SSD_N_HEADS), w.dtype)], axis=1)
    c16 = lambda a: a.astype(BF16)
    return dict(z=c16(z), xbc=c16(xbc_gm),
                q=c16(q), k=c16(k), v=c16(v), iq=c16(iq), small=c16(small),
                gates=c16(jnp.concatenate([g_ssd, g_att], axis=1)))


def _prep_ssd_params(conv_w, conv_b, dt_bias, a_log, d_skip, norm_g):
    g = SSD_N_GROUPS

    def gm_channels(a):
        r = a.shape[0]
        xs = a[:, :SSD_D_INNER].reshape(r, g, SSD_GW)
        b = a[:, SSD_D_INNER:SSD_D_INNER + g * SSD_D_STATE].reshape(r, g, SSD_D_STATE)
        c = a[:, SSD_D_INNER + g * SSD_D_STATE:].reshape(r, g, SSD_D_STATE)
        return jnp.transpose(jnp.concatenate([xs, b, c], axis=2), (1, 0, 2))

    def gm_heads(a):
        return jnp.pad(a.reshape(1, SSD_N_HEADS), ((0, 0), (0, LANES - SSD_N_HEADS)))

    return (gm_channels(conv_w), gm_channels(conv_b[None, :]), gm_heads(dt_bias),
            gm_heads(a_log), gm_heads(d_skip), norm_g.reshape(g, 1, SSD_GW))


def kernel(x, mem, positions, norm_mix_pre, norm_mix_post, w_in, ssd_conv_w, ssd_conv_b,
           ssd_dt_bias, ssd_a_log, ssd_d, ssd_norm, w_br_ssd, w_br_att, w_out,
           norm_mem_pre, norm_mem_post, mem_norm, w_mem_q, w_mem_kv, w_mem_o,
           norm_ffn_pre, norm_ffn_post, w_ffn_in, w_ffn_out):
    bsz, s_len, d = x.shape
    m = bsz * s_len
    depth = w_in.shape[0]
    x2 = x.reshape(m, d)
    mem2 = mem.reshape(bsz * mem.shape[1], d)
    rope_att, rope_idx, rope_small = _rope_tables(positions)
    rope_ex = lambda t: tuple((a, "m128") for a in t)
    tm = 1024
    gcol = D_MODEL // 1024

    gvec = lambda a: a.reshape(1, d)
    hn = rmsnorm_bf16(x2, norm_mix_pre[0])
    for l in range(depth):
        wp = _prep_in_proj(w_in[l])
        cw_g, cb_g, *ssd_par = _prep_ssd_params(ssd_conv_w[l], ssd_conv_b[l], ssd_dt_bias[l],
                                                 ssd_a_log[l], ssd_d[l], ssd_norm[l])
        zs_g = matmul(hn, wp["z"], tm=tm, tn=SSD_GW, out_gm=True, epi=_epi_silu, name="proj_z")
        xbc_g = xbc_conv_proj(hn, wp["xbc"], cw_g, cb_g, s_len=s_len, tm=tm)
        q = matmul(hn, wp["q"], tm=tm, tn=1024, out_dtype=BF16, name="proj_q",
                   epi=functools.partial(_epi_rope, ATT_HEAD_DIM // ROPE_FRACTION // 2, 1024 // LANES),
                   extras=rope_ex(rope_att))
        k = matmul(hn, wp["k"], tm=tm, tn=512, out_dtype=BF16, name="proj_k",
                   epi=functools.partial(_epi_rope, ATT_HEAD_DIM // ROPE_FRACTION // 2, 512 // LANES),
                   extras=rope_ex(rope_att))
        v = matmul(hn, wp["v"], tm=tm, tn=512, out_dtype=BF16, name="proj_v")
        iq = matmul(hn, wp["iq"], tm=tm, tn=1024, out_dtype=BF16, name="proj_iq",
                    epi=functools.partial(_epi_rope, IDX_HEAD_DIM // ROPE_FRACTION // 2, 1024 // LANES),
                    extras=rope_ex(rope_idx))
        small = matmul(hn, wp["small"], tm=tm, tn=2 * LANES, name="proj_small",
                       epi=functools.partial(_epi_rope, IDX_HEAD_DIM // ROPE_FRACTION // 2, 1),
                       extras=rope_ex(rope_small))
        gates = matmul(hn, wp["gates"], tm=tm, tn=1024, epi=_epi_sigmoid, name="proj_gates")

        y_g = ssd_mixer(zs_g, xbc_g, small, *ssd_par, bsz=bsz, s_len=s_len)
        att = dsa_mixer(q, k, v, iq, small, bsz=bsz, s_len=s_len)

        u_ssd = matmul(y_g, w_br_ssd[l].astype(BF16), tm=tm, tn=1024, a_gm="full",
                       epi=_epi_gate_mul, extras=((gates, "mn", 0),), name="branch_ssd")
        merged = matmul(att, w_br_att[l].astype(BF16), tm=tm, tn=1024, out_dtype=BF16,
                        epi=_epi_gate_mul_add, extras=((gates, "mn", gcol), (u_ssd, "mn", 0)),
                        name="branch_att_merge")
        x2, hn = matmul(merged, w_out[l].astype(BF16), tm=256, tn=D_MODEL, epi=_epi_norm_res_next,
                        extras=((gvec(norm_mix_post[l]), "n"), (x2, "mn"), (gvec(norm_mem_pre[l]), "n")),
                        out2_dtype=BF16, name="mix_out")

        qm = matmul(hn, w_mem_q[l].astype(BF16), tm=tm, tn=1024, out_dtype=BF16, name="mem_q")
        mem_n = rmsnorm_bf16(mem2, mem_norm[l])
        kvm = matmul(mem_n, w_mem_kv[l].astype(BF16), tm=tm, tn=1024, out_dtype=BF16, name="mem_kv")
        om = mem_attention(qm, kvm, bsz=bsz, s_len=s_len)
        x2, hn = matmul(om, w_mem_o[l].astype(BF16), tm=256, tn=D_MODEL, epi=_epi_norm_res_next,
                        extras=((gvec(norm_mem_post[l]), "n"), (x2, "mn"), (gvec(norm_ffn_pre[l]), "n")),
                        out2_dtype=BF16, name="mem_out")

        wf = w_ffn_in[l]
        hf = swiglu_in(hn, wf[:, :D_FF].astype(BF16), wf[:, D_FF:].astype(BF16), tm=tm, tn=512)
        w_fo = w_ffn_out[l].astype(BF16)
        if l + 1 < depth:
            x2, hn = matmul(hf, w_fo, tm=512, tn=D_MODEL, tk=D_FF // 4, epi=_epi_norm_res_next,
                            extras=((gvec(norm_ffn_post[l]), "n"), (x2, "mn"),
                                    (gvec(norm_mix_pre[l + 1]), "n")),
                            out2_dtype=BF16, name="ffn_out")
        else:
            x2 = matmul(hf, w_fo, tm=512, tn=D_MODEL, tk=D_FF // 4, epi=_epi_norm_res,
                        extras=((gvec(norm_ffn_post[l]), "n"), (x2, "mn")), name="ffn_out_last")

    return x2.reshape(bsz, s_len, d)
```

```python
import functools
import math

import jax
import jax.numpy as jnp
from jax import lax
from jax.experimental import pallas as pl
from jax.experimental.pallas import tpu as pltpu

F32 = jnp.float32
BF16 = jnp.bfloat16
I32 = jnp.int32

D_MODEL = 2048
N_MEM = 256
SSD_D_INNER = 4096
SSD_HEAD_DIM = 64
SSD_N_HEADS = 64
SSD_N_GROUPS = 8
SSD_HPG = SSD_N_HEADS // SSD_N_GROUPS
SSD_GW = SSD_D_INNER // SSD_N_GROUPS
SSD_D_STATE = 128
SSD_CONV = 4
SSD_CHUNK = 128
SSD_XBC_GW = SSD_GW + 2 * SSD_D_STATE
ATT_HEAD_DIM = 128
ATT_N_HEADS = 16
ATT_N_KV = 4
ATT_GROUP = ATT_N_HEADS // ATT_N_KV
IDX_N_HEADS = 16
IDX_HEAD_DIM = 64
TOPK_MAX = 256
ROPE_THETA = 500000.0
ROPE_FRACTION = 4
MEM_HEADS = 4
MEM_HEAD_DIM = D_MODEL // MEM_HEADS
D_FF = 5632
NORM_EPS = 1e-6
IN_PROJ_SIZES = (4096, 6144, 64, 2048, 512, 512, 1024, 64, 16, 2048, 2048)

LANES = 128
SUBLANES = 8
V7X_VMEM_BYTES = 64 * 1024 * 1024
VMEM_LIMIT = V7X_VMEM_BYTES - 8 * 1024 * 1024

INT_MIN = -(2 ** 31)
NEG_BIG = -1e30

_NT = (((1,), (1,)), ((), ()))


def _cparams(sem):
    return pltpu.CompilerParams(dimension_semantics=sem, vmem_limit_bytes=VMEM_LIMIT)


def _rmsnorm_kernel(x_ref, g_ref, o_ref):
    x = x_ref[...]
    ms = jnp.mean(x * x, axis=-1, keepdims=True)
    o_ref[...] = (x * lax.rsqrt(ms + NORM_EPS) * g_ref[...]).astype(o_ref.dtype)


def rmsnorm_bf16(x2, g, tm=256):
    m, d = x2.shape
    tm = min(tm, m)
    return pl.pallas_call(
        _rmsnorm_kernel,
        grid=(m // tm,),
        in_specs=[pl.BlockSpec((tm, d), lambda i: (i, 0)),
                  pl.BlockSpec((1, d), lambda i: (0, 0))],
        out_specs=pl.BlockSpec((tm, d), lambda i: (i, 0)),
        out_shape=jax.ShapeDtypeStruct((m, d), BF16),
        compiler_params=_cparams(("parallel",)),
        name="rmsnorm",
    )(x2, g.reshape(1, d))


def _epi_none(acc):
    return acc


def _epi_sigmoid(acc):
    return jax.nn.sigmoid(acc)


def _epi_silu(acc):
    return acc * jax.nn.sigmoid(acc)


def _epi_gate_mul(acc, gate_ref):
    return acc * gate_ref[...].astype(F32)


def _epi_gate_mul_add(acc, gate_ref, add_ref):
    return acc * gate_ref[...].astype(F32) + add_ref[...].astype(F32)


def _epi_rope(shift, n_rot, acc, c_ref, s1_ref, s2_ref):
    c = c_ref[...]
    s1 = s1_ref[...]
    s2 = s2_ref[...]
    outs = []
    for j in range(acc.shape[1] // LANES):
        sl = acc[:, j * LANES:(j + 1) * LANES]
        if j < n_rot:
            sl = (sl * c + pltpu.roll(sl, shift, 1) * s1
                  + pltpu.roll(sl, LANES - shift, 1) * s2)
        outs.append(sl)
    return outs[0] if len(outs) == 1 else jnp.concatenate(outs, axis=1)


def _epi_norm_res(acc, g_ref, x_ref):
    ms = jnp.mean(acc * acc, axis=-1, keepdims=True)
    return x_ref[...] + acc * lax.rsqrt(ms + NORM_EPS) * g_ref[...]


def _epi_norm_res_next(acc, g_ref, x_ref, gnext_ref):
    xn = _epi_norm_res(acc, g_ref, x_ref)
    ms = jnp.mean(xn * xn, axis=-1, keepdims=True)
    return xn, xn * lax.rsqrt(ms + NORM_EPS) * gnext_ref[...]


def _mm_kernel(epi, n_extra, n_out, nk, a_ref, w_ref, *refs):
    extras = refs[:n_extra]
    o_refs = refs[n_extra:n_extra + n_out]

    def finish(acc):
        vals = epi(acc, *extras)
        vals = vals if isinstance(vals, tuple) else (vals,)
        for o_ref, val in zip(o_refs, vals, strict=True):
            o_ref[...] = val.astype(o_ref.dtype).reshape(o_ref.shape)

    if len(a_ref.shape) == 3 and a_ref.shape[0] > 1:
        ng, _, gk = a_ref.shape
        part = jnp.dot(a_ref[0], w_ref[0:gk, :], preferred_element_type=F32)
        for gi in range(1, ng):
            part = part + jnp.dot(a_ref[gi], w_ref[gi * gk:(gi + 1) * gk, :],
                                  preferred_element_type=F32)
        finish(part)
        return
    a = a_ref[...]
    a = a.reshape(a.shape[-2], a.shape[-1])
    part = jnp.dot(a, w_ref[...], preferred_element_type=F32)
    if nk == 1:
        finish(part)
        return
    acc_ref = refs[n_extra + n_out]
    k = pl.program_id(2)

    @pl.when(k == 0)
    def _():
        acc_ref[...] = part

    @pl.when(k > 0)
    def _():
        acc_ref[...] += part

    @pl.when(k == nk - 1)
    def _():
        finish(acc_ref[...])


def matmul(a, w, *, tm, tn, tk=None, out_dtype=F32, epi=_epi_none, extras=(),
           a_gm=False, out_gm=False, out2_dtype=None, name="matmul"):
    kdim, n = w.shape
    if a_gm == "full":
        ng, m, gk = a.shape
        assert ng * gk == kdim
        nk, tk = 1, kdim
    elif a_gm:
        nk, m, tk = a.shape
        assert nk * tk == kdim
    else:
        m = a.shape[0]
        tk = kdim if tk is None else tk
        assert kdim % tk == 0
        nk = kdim // tk
    tm = min(tm, m)
    assert m % tm == 0 and n % tn == 0
    if a_gm == "full":
        a_spec = pl.BlockSpec((ng, tm, gk), lambda i, j, k: (0, i, 0))
    elif a_gm:
        a_spec = pl.BlockSpec((1, tm, tk), lambda i, j, k: (k, i, 0))
    else:
        a_spec = pl.BlockSpec((tm, tk), lambda i, j, k: (i, k))
    in_specs = [a_spec, pl.BlockSpec((tk, tn), lambda i, j, k: (k, j))]
    args = [a, w]
    for ex in extras:
        arr, kind = ex[0], ex[1]
        off = ex[2] if len(ex) > 2 else 0
        if kind == "mn":
            in_specs.append(pl.BlockSpec((tm, tn), lambda i, j, k, off=off: (i, j + off)))
        elif kind == "mn1":
            in_specs.append(pl.BlockSpec((tm, tn), lambda i, j, k, off=off: (i, j + off),
                                         pipeline_mode=pl.Buffered(1)))
        elif kind == "m128":
            in_specs.append(pl.BlockSpec((tm, LANES), lambda i, j, k: (i, 0)))
        elif kind == "n":
            in_specs.append(pl.BlockSpec((1, tn), lambda i, j, k: (0, j)))
        else:
            raise ValueError(kind)
        args.append(arr)
    if out_gm:
        out_spec = pl.BlockSpec((1, tm, tn), lambda i, j, k: (j, i, 0))
        out_shape = jax.ShapeDtypeStruct((n // tn, m, tn), out_dtype)
    else:
        out_spec = pl.BlockSpec((tm, tn), lambda i, j, k: (i, j))
        out_shape = jax.ShapeDtypeStruct((m, n), out_dtype)
    n_out = 1
    if out2_dtype is not None:
        assert not out_gm
        n_out = 2
        out_spec = [out_spec, pl.BlockSpec((tm, tn), lambda i, j, k: (i, j))]
        out_shape = [out_shape, jax.ShapeDtypeStruct((m, n), out2_dtype)]
    scratch = [pltpu.VMEM((tm, tn), F32)] if nk > 1 else []
    return pl.pallas_call(
        functools.partial(_mm_kernel, epi, len(extras), n_out, nk),
        grid=(m // tm, n // tn, nk),
        in_specs=in_specs,
        out_specs=out_spec,
        out_shape=out_shape,
        scratch_shapes=scratch,
        compiler_params=_cparams(("parallel", "parallel", "arbitrary")),
        name=name,
    )(*args)


def _swiglu_kernel(a_ref, wg_ref, wu_ref, o_ref):
    a = a_ref[...]
    g = jnp.dot(a, wg_ref[...], preferred_element_type=F32)
    u = jnp.dot(a, wu_ref[...], preferred_element_type=F32)
    o_ref[...] = (g * jax.nn.sigmoid(g) * u).astype(o_ref.dtype)


def swiglu_in(a, wg, wu, *, tm, tn):
    m, kdim = a.shape
    n = wg.shape[1]
    tm = min(tm, m)
    return pl.pallas_call(
        _swiglu_kernel,
        grid=(m // tm, n // tn),
        in_specs=[pl.BlockSpec((tm, kdim), lambda i, j: (i, 0)),
                  pl.BlockSpec((kdim, tn), lambda i, j: (0, j)),
                  pl.BlockSpec((kdim, tn), lambda i, j: (0, j))],
        out_specs=pl.BlockSpec((tm, tn), lambda i, j: (i, j)),
        out_shape=jax.ShapeDtypeStruct((m, n), BF16),
        compiler_params=_cparams(("parallel", "parallel")),
        name="swiglu_in",
    )(a, wg, wu)


def _xbc_conv_kernel(a_ref, w_ref, cw_ref, cb_ref, o_ref, ubuf, *, tiles_per_seq):
    i = pl.program_id(1)
    tm = a_ref.shape[0]
    acc = jnp.dot(a_ref[...], w_ref[...], preferred_element_type=F32)
    prev_tail = ubuf[tm:tm + SUBLANES, :]
    seq_start = lax.rem(i, tiles_per_seq) == 0
    ubuf[0:SUBLANES, :] = jnp.where(seq_start, 0.0, prev_tail)
    ubuf[SUBLANES:SUBLANES + tm, :] = acc
    w = cw_ref[0]
    conv = cb_ref[0]
    for k in range(SSD_CONV):
        conv = conv + ubuf[pl.ds(SUBLANES - (SSD_CONV - 1) + k, tm), :] * w[k:k + 1, :]
    o_ref[0] = (conv * jax.nn.sigmoid(conv)).astype(o_ref.dtype)


def xbc_conv_proj(a, w, cw_g, cb_g, *, s_len, tm):
    m, kdim = a.shape
    g, gw = SSD_N_GROUPS, SSD_XBC_GW
    tm = min(tm, s_len)
    assert s_len % tm == 0 and m % s_len == 0
    return pl.pallas_call(
        functools.partial(_xbc_conv_kernel, tiles_per_seq=s_len // tm),
        grid=(g, m // tm),
        in_specs=[pl.BlockSpec((tm, kdim), lambda j, i: (i, 0)),
                  pl.BlockSpec((kdim, gw), lambda j, i: (0, j)),
                  pl.BlockSpec((1, SSD_CONV, gw), lambda j, i: (j, 0, 0)),
                  pl.BlockSpec((1, 1, gw), lambda j, i: (j, 0, 0))],
        out_specs=pl.BlockSpec((1, tm, gw), lambda j, i: (j, i, 0)),
        out_shape=jax.ShapeDtypeStruct((g, m, gw), F32),
        scratch_shapes=[pltpu.VMEM((tm + SUBLANES, gw), F32)],
        compiler_params=_cparams(("arbitrary", "arbitrary")),
        name="proj_xbc_conv",
    )(a, w, cw_g, cb_g)


def _split3_bf16(x):
    h1 = x.astype(BF16)
    r1 = x - h1.astype(F32)
    h2 = r1.astype(BF16)
    h3 = (r1 - h2.astype(F32)).astype(BF16)
    return h1, h2, h3


def _split2_bf16(x):
    hi = x.astype(BF16)
    return hi, (x - hi.astype(F32)).astype(BF16)


def _ssd_kernel(zs_ref, xbc_ref, dt_ref, dtb_ref, alog_ref, d_ref, ng_ref, ex_ref,
                y_ref, st_scr, cst_scr):
    q = SSD_CHUNK
    c_idx = pl.program_id(1)

    @pl.when(c_idx == 0)
    def _():
        st_scr[...] = jnp.zeros(st_scr.shape, F32)

    row = lax.broadcasted_iota(I32, (q, q), 0)
    col = lax.broadcasted_iota(I32, (q, q), 1)
    causal = row >= col
    tril16 = causal.astype(F32).astype(BF16)
    lo = lax.broadcasted_iota(I32, (q, LANES), 1) < SSD_HEAD_DIM

    dtr = dt_ref[...] + dtb_ref[...]
    dt_all = jnp.maximum(dtr, 0.0) + jnp.log1p(jnp.exp(-jnp.abs(dtr)))
    da = dt_all * (-jnp.exp(alog_ref[...]))
    h1, h2, h3 = _split3_bf16(da)
    cs_all = (jnp.dot(tril16, h1, preferred_element_type=F32)
              + jnp.dot(tril16, h2, preferred_element_type=F32)
              + jnp.dot(tril16, h3, preferred_element_type=F32))
    cst_scr[...] = cs_all.T
    last_all = cs_all[q - 1:q, :]
    ecs_all = jnp.exp(cs_all)
    wend_all = jnp.exp(last_all - cs_all)
    elast_all = jnp.exp(last_all)
    tok_splits = [_split2_bf16(v) for v in (dt_all, ecs_all, wend_all)]
    row_vals = jnp.concatenate(
        [elast_all, d_ref[...], jnp.zeros((SUBLANES - 2, LANES), F32)], axis=0)
    row_splits = _split3_bf16(row_vals)

    def group(g, carry):
        ex = ex_ref[g]
        spread = lambda parts: sum(jnp.dot(h, ex, preferred_element_type=F32) for h in parts)
        dt_x, ecs_x, wend_x = (spread(parts) for parts in tok_splits)
        rows_x = spread(row_splits)
        el = rows_x[0:1, :]
        dsk_x = rows_x[1:2, :]
        cs = pltpu.roll(cs_all, lax.rem(LANES - SSD_HPG * g, LANES), 1)
        cs_t = cst_scr[pl.ds(pl.multiple_of(g * SSD_HPG, SSD_HPG), SSD_HPG), :]

        xs = xbc_ref[g, :, 0:SSD_GW]
        bm = xbc_ref[g, :, SSD_GW:SSD_GW + SSD_D_STATE]
        cm = xbc_ref[g, :, SSD_GW + SSD_D_STATE:SSD_XBC_GW]

        bm16 = bm.astype(BF16)
        cm16 = cm.astype(BF16)
        cb = lax.dot_general(cm16, bm16, _NT, preferred_element_type=F32)
        st_t = st_scr[g]
        ystate = jnp.dot(cm16, st_t.astype(BF16), preferred_element_type=F32)

        xdt_g = xs * dt_x
        xw = (xdt_g * wend_x).astype(BF16)
        ys = []
        for p in range(SSD_HPG // 2):
            j0 = 2 * p
            xdt = xdt_g[:, p * LANES:(p + 1) * LANES]
            lmats = []
            for j in (j0, j0 + 1):
                diff = cs[:, j:j + 1] - cs_t[j:j + 1, :]
                dec = jnp.exp(jnp.where(causal, diff, -jnp.inf))
                lmats.append((cb * dec).astype(BF16))
            lmat = jnp.concatenate(lmats, axis=1)
            rhs = jnp.concatenate([jnp.where(lo, xdt, 0.0), jnp.where(lo, 0.0, xdt)],
                                  axis=0).astype(BF16)
            ys.append(jnp.dot(lmat, rhs, preferred_element_type=F32))
        y = jnp.concatenate(ys, axis=1) + ystate * ecs_x + dsk_x * xs
        st_scr[g] = st_t * el + jnp.dot(bm.T.astype(BF16), xw, preferred_element_type=F32)

        y = y * zs_ref[g]
        y = y * lax.rsqrt(jnp.mean(y * y, axis=-1, keepdims=True) + NORM_EPS) * ng_ref[g]
        y_ref[g] = y.astype(y_ref.dtype)
        return carry

    lax.fori_loop(0, SSD_N_GROUPS, group, 0)


def _head_spread_table():
    head = lax.broadcasted_iota(I32, (SSD_N_GROUPS, LANES, SSD_GW), 1)
    grp = lax.broadcasted_iota(I32, (SSD_N_GROUPS, LANES, SSD_GW), 0)
    chan = lax.broadcasted_iota(I32, (SSD_N_GROUPS, LANES, SSD_GW), 2)
    return (head == grp * SSD_HPG + chan // SSD_HEAD_DIM).astype(BF16)


def ssd_mixer(zs_g, xbc_g, small, dtb, alog, dskip, ng_g, *, bsz, s_len):
    g, q = SSD_N_GROUPS, SSD_CHUNK
    m = bsz * s_len
    nc = s_len // q

    def act(w):
        return pl.BlockSpec((g, q, w), lambda b, c: (0, b * nc + c, 0))

    def par(r, w):
        return pl.BlockSpec((g, r, w), lambda b, c: (0, 0, 0))

    head = pl.BlockSpec((1, LANES), lambda b, c: (0, 0))

    return pl.pallas_call(
        _ssd_kernel,
        grid=(bsz, nc),
        in_specs=[act(SSD_GW), act(SSD_XBC_GW),
                  pl.BlockSpec((q, LANES), lambda b, c: (b * nc + c, 1)),
                  head, head, head, par(1, SSD_GW), par(LANES, SSD_GW)],
        out_specs=act(SSD_GW),
        out_shape=jax.ShapeDtypeStruct((g, m, SSD_GW), BF16),
        scratch_shapes=[pltpu.VMEM((g, SSD_D_STATE, SSD_GW), F32),
                        pltpu.VMEM((LANES, q), F32)],
        compiler_params=_cparams(("arbitrary", "arbitrary")),
        name="ssd_mixer",
    )(zs_g, xbc_g, small, dtb, alog, dskip, ng_g, _head_spread_table())


def _dsa_kernel(q_ref, iq_ref, smq_ref, k_ref, v_ref, smk_ref, o_ref,
                key_scr, s_scr, mx_scr, l_scr, acc_scr, *, tq, ck, topk):
    i = pl.program_id(1)
    q0 = i * tq
    nch = lax.div(q0 + tq + ck - 1, ck)
    idx_scale = (IDX_HEAD_DIM ** -0.5) * (IDX_N_HEADS ** -0.5)
    att_scale = ATT_HEAD_DIM ** -0.5

    iw = smq_ref[0][:, IDX_HEAD_DIM:IDX_HEAD_DIM + IDX_N_HEADS]
    tpos = q0 + lax.broadcasted_iota(I32, (tq, ck), 0)
    lane_pos = lax.broadcasted_iota(I32, (tq, ck), 1)

    def score_chunk(c, carry):
        s0 = pl.multiple_of(c * ck, ck)
        ikc = smk_ref[0, pl.ds(s0, ck), :][:, :IDX_HEAD_DIM].astype(BF16)
        acc = jnp.zeros((tq, ck), F32)
        for h in range(IDX_N_HEADS):
            iqh = iq_ref[0, :, h * IDX_HEAD_DIM:(h + 1) * IDX_HEAD_DIM]
            d = lax.dot_general(iqh, ikc, _NT, preferred_element_type=F32)
            acc = acc + jnp.maximum(d, 0.0) * iw[:, h:h + 1]
        acc = acc * idx_scale
        bits = pltpu.bitcast(acc, I32)
        key = jnp.where(bits < 0, bits ^ jnp.int32(0x7FFFFFFF), bits)
        key = jnp.where(key == -1, 0, key)
        key_scr[c] = jnp.where(s0 + lane_pos <= tpos, key, jnp.int32(INT_MIN))
        return carry

    lax.fori_loop(0, nch, score_chunk, 0)

    def count_ge(cand):
        cand_b = jnp.broadcast_to(cand, (tq, LANES))

        def body(c, cnt):
            kc = key_scr[c]
            for u in range(ck // LANES):
                cnt = cnt + (kc[:, u * LANES:(u + 1) * LANES] >= cand_b).astype(I32)
            return cnt

        cnt = lax.fori_loop(0, nch, body, jnp.zeros((tq, LANES), I32))
        return jnp.sum(cnt.astype(F32), axis=1, keepdims=True)

    keep_all = q0 + lax.broadcasted_iota(I32, (tq, 1), 0) < topk

    def bis_cond(st):
        it, _, _, pending = st
        return jnp.logical_and(it < 32, pending > 0.0)

    def bis_step(st):
        it, thr, cnt_thr, _ = st
        cand = thr + lax.shift_left(jnp.int32(1), 31 - it)
        cnt = count_ge(cand)
        take = cnt >= float(topk)
        thr = jnp.where(take, cand, thr)
        cnt_thr = jnp.where(take, cnt, cnt_thr)
        done = jnp.logical_or(cnt_thr == float(topk), keep_all)
        return it + 1, thr, cnt_thr, jnp.sum(jnp.where(done, 0.0, 1.0))

    _, thr, cnt_thr, _ = lax.while_loop(
        bis_cond, bis_step,
        (jnp.int32(0), jnp.full((tq, 1), INT_MIN, I32), jnp.full((tq, 1), 2.0 * topk, F32),
         jnp.float32(1.0)))

    tied = jnp.logical_and(cnt_thr > float(topk), jnp.logical_not(keep_all))

    @pl.when(jnp.sum(jnp.where(tied, 1.0, 0.0)) > 0.0)
    def _():
        thr_t = jnp.broadcast_to(thr, (tq, LANES))
        lane = lax.broadcasted_iota(I32, (tq, LANES), 1)

        def count(pred):
            def body(c, cnt):
                kc = key_scr[c]
                for u in range(ck // LANES):
                    hit = pred(kc[:, u * LANES:(u + 1) * LANES], c * ck + u * LANES + lane)
                    cnt = cnt + hit.astype(I32)
                return cnt

            cnt = lax.fori_loop(0, nch, body, jnp.zeros((tq, LANES), I32))
            return jnp.sum(cnt.astype(F32), axis=1, keepdims=True)

        need = float(topk) - count(lambda kv, pos: kv > thr_t)
        nbits = (key_scr.shape[0] * ck).bit_length()

        def pos_step(it, below):
            cand = below + lax.shift_left(jnp.int32(1), nbits - 1 - it)
            cand_b = jnp.broadcast_to(cand, (tq, LANES))
            n = count(lambda kv, pos: jnp.logical_and(kv == thr_t, pos < cand_b))
            return jnp.where(n < need, cand, below)

        below = lax.fori_loop(0, nbits, pos_step, jnp.zeros((tq, 1), I32))
        first_dropped = jnp.where(tied, below + 1, jnp.int32(2 ** 30))
        drop_b = jnp.broadcast_to(first_dropped, (tq, LANES))

        def retire(c, carry):
            kc = key_scr[c]
            outs = []
            for u in range(ck // LANES):
                kv = kc[:, u * LANES:(u + 1) * LANES]
                drop = jnp.logical_and(kv == thr_t, c * ck + u * LANES + lane >= drop_b)
                outs.append(jnp.where(drop, jnp.int32(INT_MIN), kv))
            key_scr[c] = jnp.concatenate(outs, axis=1)
            return carry

        lax.fori_loop(0, nch, retire, 0)

    thr = jnp.where(keep_all, jnp.int32(INT_MIN + 1), jnp.maximum(thr, jnp.int32(INT_MIN + 1)))
    thr_b = jnp.broadcast_to(thr, (tq, LANES))

    rows = ATT_GROUP * tq
    exp2_scale = att_scale * math.log2(math.e)

    def stacked_q(g):
        return jnp.concatenate(
            [q_ref[0, :, (g * ATT_GROUP + hh) * ATT_HEAD_DIM:(g * ATT_GROUP + hh + 1) * ATT_HEAD_DIM]
             for hh in range(ATT_GROUP)], axis=0)

    def logits_pass(g, qg, chunks):
        ksl = slice(g * ATT_HEAD_DIM, (g + 1) * ATT_HEAD_DIM)
        mxs = [mx_scr[g % 2, hh * tq:(hh + 1) * tq, :] for hh in range(ATT_GROUP)]
        for c in chunks:
            s0 = pl.multiple_of(c * ck, ck)
            s = lax.dot_general(qg, k_ref[0, pl.ds(s0, ck), ksl], _NT,
                                preferred_element_type=F32)
            kc = key_scr[c]
            sel = [kc[:, u * LANES:(u + 1) * LANES] >= thr_b for u in range(ck // LANES)]
            for hh in range(ATT_GROUP):
                rsl = slice(hh * tq, (hh + 1) * tq)
                for u in range(ck // LANES):
                    lsl = slice(u * LANES, (u + 1) * LANES)
                    sm = jnp.where(sel[u], s[rsl, lsl], NEG_BIG)
                    s_scr[g % 2, c, rsl, lsl] = sm
                    mxs[hh] = jnp.maximum(mxs[hh], sm)
        for hh in range(ATT_GROUP):
            mx_scr[g % 2, hh * tq:(hh + 1) * tq, :] = mxs[hh]

    def pv_pass(g, m_b, chunks):
        ksl = slice(g * ATT_HEAD_DIM, (g + 1) * ATT_HEAD_DIM)
        lsum = l_scr[...]
        acc = acc_scr[...]
        for c in chunks:
            s0 = pl.multiple_of(c * ck, ck)
            ps = []
            for u in range(ck // LANES):
                p = jnp.exp2((s_scr[g % 2, c, :, u * LANES:(u + 1) * LANES] - m_b) * exp2_scale)
                lsum = lsum + p
                ps.append(p.astype(BF16))
            acc = acc + jnp.dot(jnp.concatenate(ps, axis=1), v_ref[0, pl.ds(s0, ck), ksl],
                                preferred_element_type=F32)
        l_scr[...] = lsum
        acc_scr[...] = acc

    def chunk_loop(body):
        def pair(t, carry):
            body((2 * t, 2 * t + 1))
            return carry

        lax.fori_loop(0, lax.shift_right_logical(nch, 1), pair, 0)

        @pl.when(lax.rem(nch, 2) == 1)
        def _():
            body((nch - 1,))

    mx_scr[...] = jnp.full(mx_scr.shape, NEG_BIG, F32)
    q_cur = stacked_q(0)
    chunk_loop(lambda chunks: logits_pass(0, q_cur, chunks))
    for g in range(ATT_N_KV):
        m_b = jnp.broadcast_to(jnp.max(mx_scr[g % 2], axis=1, keepdims=True), (rows, LANES))
        l_scr[...] = jnp.zeros(l_scr.shape, F32)
        acc_scr[...] = jnp.zeros(acc_scr.shape, F32)
        if g + 1 < ATT_N_KV:
            mx_scr[(g + 1) % 2] = jnp.full((rows, LANES), NEG_BIG, F32)
            q_nxt = stacked_q(g + 1)

            def fused(chunks, g=g, m_b=m_b, q_nxt=q_nxt):
                pv_pass(g, m_b, chunks)
                logits_pass(g + 1, q_nxt, chunks)
        else:
            def fused(chunks, g=g, m_b=m_b):
                pv_pass(g, m_b, chunks)

        chunk_loop(fused)
        o = acc_scr[...] / jnp.sum(l_scr[...], axis=1, keepdims=True)
        for hh in range(ATT_GROUP):
            h = g * ATT_GROUP + hh
            o_ref[0, :, h * ATT_HEAD_DIM:(h + 1) * ATT_HEAD_DIM] = (
                o[hh * tq:(hh + 1) * tq]).astype(o_ref.dtype)


def dsa_mixer(q, k, v, iq, small, *, bsz, s_len, tq=128, ck=512):
    topk = min(TOPK_MAX, s_len // 4)
    ck = min(ck, s_len)
    assert ck >= topk and s_len % ck == 0 and s_len % tq == 0
    nq = s_len // tq
    hq = ATT_N_HEADS * ATT_HEAD_DIM
    hk = ATT_N_KV * ATT_HEAD_DIM
    hi = IDX_N_HEADS * IDX_HEAD_DIM
    r3 = lambda a: a.reshape(bsz, s_len, a.shape[-1])
    out = pl.pallas_call(
        functools.partial(_dsa_kernel, tq=tq, ck=ck, topk=topk),
        grid=(bsz, nq),
        in_specs=[pl.BlockSpec((1, tq, hq), lambda b, i: (b, i, 0)),
                  pl.BlockSpec((1, tq, hi), lambda b, i: (b, i, 0)),
                  pl.BlockSpec((1, tq, LANES), lambda b, i: (b, i, 0)),
                  pl.BlockSpec((1, s_len, hk), lambda b, i: (b, 0, 0)),
                  pl.BlockSpec((1, s_len, hk), lambda b, i: (b, 0, 0)),
                  pl.BlockSpec((1, s_len, LANES), lambda b, i: (b, 0, 0))],
        out_specs=pl.BlockSpec((1, tq, hq), lambda b, i: (b, i, 0)),
        out_shape=jax.ShapeDtypeStruct((bsz, s_len, hq), BF16),
        scratch_shapes=[pltpu.VMEM((s_len // ck, tq, ck), I32),
                        pltpu.VMEM((2, s_len // ck, ATT_GROUP * tq, ck), F32),
                        pltpu.VMEM((2, ATT_GROUP * tq, LANES), F32),
                        pltpu.VMEM((ATT_GROUP * tq, LANES), F32),
                        pltpu.VMEM((ATT_GROUP * tq, ATT_HEAD_DIM), F32)],
        compiler_params=_cparams(("parallel", "arbitrary")),
        name="dsa_mixer",
    )(r3(q), r3(iq), r3(small), r3(k), r3(v), r3(small))
    return out.reshape(bsz * s_len, hq)


def _memattn_kernel(q_ref, kv_ref, o_ref):
    scale = MEM_HEAD_DIM ** -0.5
    for h in range(MEM_HEADS):
        sl = slice(h * MEM_HEAD_DIM, (h + 1) * MEM_HEAD_DIM)
        qh = q_ref[:, sl]
        kh = kv_ref[:, sl]
        vh = kv_ref[:, D_MODEL + h * MEM_HEAD_DIM:D_MODEL + (h + 1) * MEM_HEAD_DIM]
        s = lax.dot_general(qh, kh, _NT, preferred_element_type=F32) * scale
        p = jnp.exp(s - jnp.max(s, axis=1, keepdims=True))
        o = jnp.dot(p.astype(BF16), vh, preferred_element_type=F32)
        o_ref[:, sl] = (o / jnp.sum(p, axis=1, keepdims=True)).astype(o_ref.dtype)


def mem_attention(qm, kv, *, bsz, s_len, tq=512):
    m = bsz * s_len
    tq = min(tq, s_len)
    nq = s_len // tq
    n_mem = kv.shape[0] // bsz
    return pl.pallas_call(
        _memattn_kernel,
        grid=(bsz, nq),
        in_specs=[pl.BlockSpec((tq, D_MODEL), lambda b, i: (b * nq + i, 0)),
                  pl.BlockSpec((n_mem, 2 * D_MODEL), lambda b, i: (b, 0))],
        out_specs=pl.BlockSpec((tq, D_MODEL), lambda b, i: (b * nq + i, 0)),
        out_shape=jax.ShapeDtypeStruct((m, D_MODEL), BF16),
        compiler_params=_cparams(("parallel", "parallel")),
        name="mem_attention",
    )(qm, kv)


def _rope_tables(positions):
    m = positions.size
    pos = positions.astype(F32).reshape(m, 1)

    def unit(head_dim):
        rot = head_dim // ROPE_FRACTION
        half = rot // 2
        inv_freq = jnp.power(ROPE_THETA, -jnp.arange(half, dtype=F32) * 2.0 / rot)
        ang = pos * inv_freq
        cos, sin = jnp.cos(ang), jnp.sin(ang)
        one = jnp.ones((m, head_dim - rot), F32)
        zero = jnp.zeros((m, head_dim - rot), F32)
        zh = jnp.zeros((m, half), F32)
        c = jnp.concatenate([cos, cos, one], axis=1)
        s1 = jnp.concatenate([zh, sin, zero], axis=1)
        s2 = jnp.concatenate([-sin, zh, zero], axis=1)
        return c, s1, s2

    att = unit(ATT_HEAD_DIM)
    ic, is1, is2 = unit(IDX_HEAD_DIM)
    idx = tuple(jnp.concatenate([t, t], axis=1) for t in (ic, is1, is2))
    pad1 = jnp.ones((m, LANES - IDX_HEAD_DIM), F32)
    pad0 = jnp.zeros((m, LANES - IDX_HEAD_DIM), F32)
    small = (jnp.concatenate([ic, pad1], axis=1),
             jnp.concatenate([is1, pad0], axis=1),
             jnp.concatenate([is2, pad0], axis=1))
    return att, idx, small


def _prep_in_proj(w):
    offs = [0]
    for s in IN_PROJ_SIZES:
        offs.append(offs[-1] + s)
    seg = lambda i: w[:, offs[i]:offs[i + 1]]
    z, xbc, dt, q, k, v, iq, ik, iw, g_ssd, g_att = (seg(i) for i in range(11))
    d = w.shape[0]
    g = SSD_N_GROUPS
    xs_w = xbc[:, :SSD_D_INNER].reshape(d, g, SSD_GW)
    b_w = xbc[:, SSD_D_INNER:SSD_D_INNER + g * SSD_D_STATE].reshape(d, g, SSD_D_STATE)
    c_w = xbc[:, SSD_D_INNER + g * SSD_D_STATE:].reshape(d, g, SSD_D_STATE)
    xbc_gm = jnp.concatenate([xs_w, b_w, c_w], axis=2).reshape(d, g * SSD_XBC_GW)
    small = jnp.concatenate(
        [ik, iw, jnp.zeros((d, LANES - IDX_HEAD_DIM - IDX_N_HEADS), w.dtype),
         dt, jnp.zeros((d, LANES - SSD_N_HEADS), w.dtype)], axis=1)
    c16 = lambda a: a.astype(BF16)
    return dict(z=c16(z), xbc=c16(xbc_gm),
                q=c16(q), k=c16(k), v=c16(v), iq=c16(iq), small=c16(small),
                gates=c16(jnp.concatenate([g_ssd, g_att], axis=1)))


def _prep_ssd_params(conv_w, conv_b, dt_bias, a_log, d_skip, norm_g):
    g = SSD_N_GROUPS

    def gm_channels(a):
        r = a.shape[0]
        xs = a[:, :SSD_D_INNER].reshape(r, g, SSD_GW)
        b = a[:, SSD_D_INNER:SSD_D_INNER + g * SSD_D_STATE].reshape(r, g, SSD_D_STATE)
        c = a[:, SSD_D_INNER + g * SSD_D_STATE:].reshape(r, g, SSD_D_STATE)
        return jnp.transpose(jnp.concatenate([xs, b, c], axis=2), (1, 0, 2))

    def gm_heads(a):
        return jnp.pad(a.reshape(1, SSD_N_HEADS), ((0, 0), (0, LANES - SSD_N_HEADS)))

    return (gm_channels(conv_w), gm_channels(conv_b[None, :]), gm_heads(dt_bias),
            gm_heads(a_log), gm_heads(d_skip), norm_g.reshape(g, 1, SSD_GW))


def kernel(x, mem, positions, norm_mix_pre, norm_mix_post, w_in, ssd_conv_w, ssd_conv_b,
           ssd_dt_bias, ssd_a_log, ssd_d, ssd_norm, w_br_ssd, w_br_att, w_out,
           norm_mem_pre, norm_mem_post, mem_norm, w_mem_q, w_mem_kv, w_mem_o,
           norm_ffn_pre, norm_ffn_post, w_ffn_in, w_ffn_out):
    bsz, s_len, d = x.shape
    m = bsz * s_len
    depth = w_in.shape[0]
    x2 = x.reshape(m, d)
    mem2 = mem.reshape(bsz * mem.shape[1], d)
    rope_att, rope_idx, rope_small = _rope_tables(positions)
    rope_ex = lambda t: tuple((a, "m128") for a in t)
    tm = 1024
    gcol = D_MODEL // 1024

    gvec = lambda a: a.reshape(1, d)
    hn = rmsnorm_bf16(x2, norm_mix_pre[0])
    for l in range(depth):
        wp = _prep_in_proj(w_in[l])
        cw_g, cb_g, *ssd_par = _prep_ssd_params(ssd_conv_w[l], ssd_conv_b[l], ssd_dt_bias[l],
                                                 ssd_a_log[l], ssd_d[l], ssd_norm[l])
        zs_g = matmul(hn, wp["z"], tm=tm, tn=SSD_GW, out_gm=True, epi=_epi_silu, name="proj_z")
        xbc_g = xbc_conv_proj(hn, wp["xbc"], cw_g, cb_g, s_len=s_len, tm=tm)
        q = matmul(hn, wp["q"], tm=tm, tn=1024, out_dtype=BF16, name="proj_q",
                   epi=functools.partial(_epi_rope, ATT_HEAD_DIM // ROPE_FRACTION // 2, 1024 // LANES),
                   extras=rope_ex(rope_att))
        k = matmul(hn, wp["k"], tm=tm, tn=512, out_dtype=BF16, name="proj_k",
                   epi=functools.partial(_epi_rope, ATT_HEAD_DIM // ROPE_FRACTION // 2, 512 // LANES),
                   extras=rope_ex(rope_att))
        v = matmul(hn, wp["v"], tm=tm, tn=512, out_dtype=BF16, name="proj_v")
        iq = matmul(hn, wp["iq"], tm=tm, tn=1024, out_dtype=BF16, name="proj_iq",
                    epi=functools.partial(_epi_rope, IDX_HEAD_DIM // ROPE_FRACTION // 2, 1024 // LANES),
                    extras=rope_ex(rope_idx))
        small = matmul(hn, wp["small"], tm=tm, tn=2 * LANES, name="proj_small",
                       epi=functools.partial(_epi_rope, IDX_HEAD_DIM // ROPE_FRACTION // 2, 1),
                       extras=rope_ex(rope_small))
        gates = matmul(hn, wp["gates"], tm=tm, tn=1024, epi=_epi_sigmoid, name="proj_gates")

        y_g = ssd_mixer(zs_g, xbc_g, small, *ssd_par, bsz=bsz, s_len=s_len)
        att = dsa_mixer(q, k, v, iq, small, bsz=bsz, s_len=s_len)

        u_ssd = matmul(y_g, w_br_ssd[l].astype(BF16), tm=tm, tn=1024, a_gm="full",
                       epi=_epi_gate_mul, extras=((gates, "mn", 0),), name="branch_ssd")
        merged = matmul(att, w_br_att[l].astype(BF16), tm=tm, tn=1024, out_dtype=BF16,
                        epi=_epi_gate_mul_add, extras=((gates, "mn", gcol), (u_ssd, "mn", 0)),
                        name="branch_att_merge")
        x2, hn = matmul(merged, w_out[l].astype(BF16), tm=256, tn=D_MODEL, epi=_epi_norm_res_next,
                        extras=((gvec(norm_mix_post[l]), "n"), (x2, "mn"), (gvec(norm_mem_pre[l]), "n")),
                        out2_dtype=BF16, name="mix_out")

        qm = matmul(hn, w_mem_q[l].astype(BF16), tm=tm, tn=1024, out_dtype=BF16, name="mem_q")
        mem_n = rmsnorm_bf16(mem2, mem_norm[l])
        kvm = matmul(mem_n, w_mem_kv[l].astype(BF16), tm=tm, tn=1024, out_dtype=BF16, name="mem_kv")
        om = mem_attention(qm, kvm, bsz=bsz, s_len=s_len)
        x2, hn = matmul(om, w_mem_o[l].astype(BF16), tm=256, tn=D_MODEL, epi=_epi_norm_res_next,
                        extras=((gvec(norm_mem_post[l]), "n"), (x2, "mn"), (gvec(norm_ffn_pre[l]), "n")),
                        out2_dtype=BF16, name="mem_out")

        wf = w_ffn_in[l]
        hf = swiglu_in(hn, wf[:, :D_FF].astype(BF16), wf[:, D_FF:].astype(BF16), tm=tm, tn=512)
        w_fo = w_ffn_out[l].astype(BF16)
        if l + 1 < depth:
            x2, hn = matmul(hf, w_fo, tm=1024, tn=D_MODEL, tk=512, epi=_epi_norm_res_next,
                            extras=((gvec(norm_ffn_post[l]), "n"), (x2, "mn1"),
                                    (gvec(norm_mix_pre[l + 1]), "n")),
                            out2_dtype=BF16, name="ffn_out")
        else:
            x2 = matmul(hf, w_fo, tm=1024, tn=D_MODEL, tk=512, epi=_epi_norm_res,
                        extras=((gvec(norm_ffn_post[l]), "n"), (x2, "mn1")), name="ffn_out_last")

    return x2.reshape(bsz, s_len, d)
```

```python
import functools
import math

import jax
import jax.numpy as jnp
from jax import lax
from jax.experimental import pallas as pl
from jax.experimental.pallas import tpu as pltpu

F32 = jnp.float32
BF16 = jnp.bfloat16
I32 = jnp.int32

D_MODEL = 2048
N_MEM = 256
SSD_D_INNER = 4096
SSD_HEAD_DIM = 64
SSD_N_HEADS = 64
SSD_N_GROUPS = 8
SSD_HPG = SSD_N_HEADS // SSD_N_GROUPS
SSD_GW = SSD_D_INNER // SSD_N_GROUPS
SSD_D_STATE = 128
SSD_CONV = 4
SSD_CHUNK = 128
SSD_XBC_GW = SSD_GW + 2 * SSD_D_STATE
ATT_HEAD_DIM = 128
ATT_N_HEADS = 16
ATT_N_KV = 4
ATT_GROUP = ATT_N_HEADS // ATT_N_KV
IDX_N_HEADS = 16
IDX_HEAD_DIM = 64
TOPK_MAX = 256
ROPE_THETA = 500000.0
ROPE_FRACTION = 4
MEM_HEADS = 4
MEM_HEAD_DIM = D_MODEL // MEM_HEADS
D_FF = 5632
NORM_EPS = 1e-6
IN_PROJ_SIZES = (4096, 6144, 64, 2048, 512, 512, 1024, 64, 16, 2048, 2048)

LANES = 128
SUBLANES = 8
V7X_VMEM_BYTES = 64 * 1024 * 1024
VMEM_LIMIT = V7X_VMEM_BYTES - 8 * 1024 * 1024

INT_MIN = -(2 ** 31)
NEG_BIG = -1e30

_NT = (((1,), (1,)), ((), ()))


def _cparams(sem):
    return pltpu.CompilerParams(dimension_semantics=sem, vmem_limit_bytes=VMEM_LIMIT)


def _rmsnorm_kernel(x_ref, g_ref, o_ref):
    x = x_ref[...]
    ms = jnp.mean(x * x, axis=-1, keepdims=True)
    o_ref[...] = (x * lax.rsqrt(ms + NORM_EPS) * g_ref[...]).astype(o_ref.dtype)


def rmsnorm_bf16(x2, g, tm=256):
    m, d = x2.shape
    tm = min(tm, m)
    return pl.pallas_call(
        _rmsnorm_kernel,
        grid=(m // tm,),
        in_specs=[pl.BlockSpec((tm, d), lambda i: (i, 0)),
                  pl.BlockSpec((1, d), lambda i: (0, 0))],
        out_specs=pl.BlockSpec((tm, d), lambda i: (i, 0)),
        out_shape=jax.ShapeDtypeStruct((m, d), BF16),
        compiler_params=_cparams(("parallel",)),
        name="rmsnorm",
    )(x2, g.reshape(1, d))


def _epi_none(acc):
    return acc


def _epi_sigmoid(acc):
    return jax.nn.sigmoid(acc)


def _epi_silu(acc):
    return acc * jax.nn.sigmoid(acc)


def _epi_gate_mul(acc, gate_ref):
    return acc * gate_ref[...].astype(F32)


def _epi_gate_mul_add(acc, gate_ref, add_ref):
    return acc * gate_ref[...].astype(F32) + add_ref[...].astype(F32)


def _epi_rope(shift, n_rot, acc, c_ref, s1_ref, s2_ref):
    c = c_ref[...]
    s1 = s1_ref[...]
    s2 = s2_ref[...]
    outs = []
    for j in range(acc.shape[1] // LANES):
        sl = acc[:, j * LANES:(j + 1) * LANES]
        if j < n_rot:
            sl = (sl * c + pltpu.roll(sl, shift, 1) * s1
                  + pltpu.roll(sl, LANES - shift, 1) * s2)
        outs.append(sl)
    return outs[0] if len(outs) == 1 else jnp.concatenate(outs, axis=1)


def _epi_norm_res(acc, g_ref, x_ref):
    ms = jnp.mean(acc * acc, axis=-1, keepdims=True)
    return x_ref[...] + acc * lax.rsqrt(ms + NORM_EPS) * g_ref[...]


def _epi_norm_res_next(acc, g_ref, x_ref, gnext_ref):
    xn = _epi_norm_res(acc, g_ref, x_ref)
    ms = jnp.mean(xn * xn, axis=-1, keepdims=True)
    return xn, xn * lax.rsqrt(ms + NORM_EPS) * gnext_ref[...]


def _mm_kernel(epi, n_extra, n_out, nk, a_ref, w_ref, *refs):
    extras = refs[:n_extra]
    o_refs = refs[n_extra:n_extra + n_out]

    def finish(acc):
        vals = epi(acc, *extras)
        vals = vals if isinstance(vals, tuple) else (vals,)
        for o_ref, val in zip(o_refs, vals, strict=True):
            o_ref[...] = val.astype(o_ref.dtype).reshape(o_ref.shape)

    if len(a_ref.shape) == 3 and a_ref.shape[0] > 1:
        ng, _, gk = a_ref.shape
        part = jnp.dot(a_ref[0], w_ref[0:gk, :], preferred_element_type=F32)
        for gi in range(1, ng):
            part = part + jnp.dot(a_ref[gi], w_ref[gi * gk:(gi + 1) * gk, :],
                                  preferred_element_type=F32)
        finish(part)
        return
    a = a_ref[...]
    a = a.reshape(a.shape[-2], a.shape[-1])
    part = jnp.dot(a, w_ref[...], preferred_element_type=F32)
    if nk == 1:
        finish(part)
        return
    acc_ref = refs[n_extra + n_out]
    k = pl.program_id(2)

    @pl.when(k == 0)
    def _():
        acc_ref[...] = part

    @pl.when(k > 0)
    def _():
        acc_ref[...] += part

    @pl.when(k == nk - 1)
    def _():
        finish(acc_ref[...])


def matmul(a, w, *, tm, tn, tk=None, out_dtype=F32, epi=_epi_none, extras=(),
           a_gm=False, out_gm=False, out2_dtype=None, name="matmul"):
    kdim, n = w.shape
    if a_gm == "full":
        ng, m, gk = a.shape
        assert ng * gk == kdim
        nk, tk = 1, kdim
    elif a_gm:
        nk, m, tk = a.shape
        assert nk * tk == kdim
    else:
        m = a.shape[0]
        tk = kdim if tk is None else tk
        assert kdim % tk == 0
        nk = kdim // tk
    tm = min(tm, m)
    assert m % tm == 0 and n % tn == 0
    if a_gm == "full":
        a_spec = pl.BlockSpec((ng, tm, gk), lambda i, j, k: (0, i, 0))
    elif a_gm:
        a_spec = pl.BlockSpec((1, tm, tk), lambda i, j, k: (k, i, 0))
    else:
        a_spec = pl.BlockSpec((tm, tk), lambda i, j, k: (i, k))
    in_specs = [a_spec, pl.BlockSpec((tk, tn), lambda i, j, k: (k, j))]
    args = [a, w]
    for ex in extras:
        arr, kind = ex[0], ex[1]
        off = ex[2] if len(ex) > 2 else 0
        if kind == "mn":
            in_specs.append(pl.BlockSpec((tm, tn), lambda i, j, k, off=off: (i, j + off)))
        elif kind == "m128":
            in_specs.append(pl.BlockSpec((tm, LANES), lambda i, j, k: (i, 0)))
        elif kind == "n":
            in_specs.append(pl.BlockSpec((1, tn), lambda i, j, k: (0, j)))
        else:
            raise ValueError(kind)
        args.append(arr)
    if out_gm:
        out_spec = pl.BlockSpec((1, tm, tn), lambda i, j, k: (j, i, 0))
        out_shape = jax.ShapeDtypeStruct((n // tn, m, tn), out_dtype)
    else:
        out_spec = pl.BlockSpec((tm, tn), lambda i, j, k: (i, j))
        out_shape = jax.ShapeDtypeStruct((m, n), out_dtype)
    n_out = 1
    if out2_dtype is not None:
        assert not out_gm
        n_out = 2
        out_spec = [out_spec, pl.BlockSpec((tm, tn), lambda i, j, k: (i, j))]
        out_shape = [out_shape, jax.ShapeDtypeStruct((m, n), out2_dtype)]
    scratch = [pltpu.VMEM((tm, tn), F32)] if nk > 1 else []
    return pl.pallas_call(
        functools.partial(_mm_kernel, epi, len(extras), n_out, nk),
        grid=(m // tm, n // tn, nk),
        in_specs=in_specs,
        out_specs=out_spec,
        out_shape=out_shape,
        scratch_shapes=scratch,
        compiler_params=_cparams(("parallel", "parallel", "arbitrary")),
        name=name,
    )(*args)


def _swiglu_kernel(a_ref, wg_ref, wu_ref, o_ref):
    a = a_ref[...]
    g = jnp.dot(a, wg_ref[...], preferred_element_type=F32)
    u = jnp.dot(a, wu_ref[...], preferred_element_type=F32)
    o_ref[...] = (g * jax.nn.sigmoid(g) * u).astype(o_ref.dtype)


def swiglu_in(a, wg, wu, *, tm, tn):
    m, kdim = a.shape
    n = wg.shape[1]
    tm = min(tm, m)
    return pl.pallas_call(
        _swiglu_kernel,
        grid=(m // tm, n // tn),
        in_specs=[pl.BlockSpec((tm, kdim), lambda i, j: (i, 0)),
                  pl.BlockSpec((kdim, tn), lambda i, j: (0, j)),
                  pl.BlockSpec((kdim, tn), lambda i, j: (0, j))],
        out_specs=pl.BlockSpec((tm, tn), lambda i, j: (i, j)),
        out_shape=jax.ShapeDtypeStruct((m, n), BF16),
        compiler_params=_cparams(("parallel", "parallel")),
        name="swiglu_in",
    )(a, wg, wu)


def _xbc_conv_kernel(a_ref, w_ref, cw_ref, cb_ref, o_ref, ubuf, *, tiles_per_seq):
    i = pl.program_id(1)
    tm = a_ref.shape[0]
    acc = jnp.dot(a_ref[...], w_ref[...], preferred_element_type=F32)
    prev_tail = ubuf[tm:tm + SUBLANES, :]
    seq_start = lax.rem(i, tiles_per_seq) == 0
    ubuf[0:SUBLANES, :] = jnp.where(seq_start, 0.0, prev_tail)
    ubuf[SUBLANES:SUBLANES + tm, :] = acc
    w = cw_ref[0]
    conv = cb_ref[0]
    for k in range(SSD_CONV):
        conv = conv + ubuf[pl.ds(SUBLANES - (SSD_CONV - 1) + k, tm), :] * w[k:k + 1, :]
    o_ref[0] = (conv * jax.nn.sigmoid(conv)).astype(o_ref.dtype)


def xbc_conv_proj(a, w, cw_g, cb_g, *, s_len, tm):
    m, kdim = a.shape
    g, gw = SSD_N_GROUPS, SSD_XBC_GW
    tm = min(tm, s_len)
    assert s_len % tm == 0 and m % s_len == 0
    return pl.pallas_call(
        functools.partial(_xbc_conv_kernel, tiles_per_seq=s_len // tm),
        grid=(g, m // tm),
        in_specs=[pl.BlockSpec((tm, kdim), lambda j, i: (i, 0)),
                  pl.BlockSpec((kdim, gw), lambda j, i: (0, j)),
                  pl.BlockSpec((1, SSD_CONV, gw), lambda j, i: (j, 0, 0)),
                  pl.BlockSpec((1, 1, gw), lambda j, i: (j, 0, 0))],
        out_specs=pl.BlockSpec((1, tm, gw), lambda j, i: (j, i, 0)),
        out_shape=jax.ShapeDtypeStruct((g, m, gw), F32),
        scratch_shapes=[pltpu.VMEM((tm + SUBLANES, gw), F32)],
        compiler_params=_cparams(("arbitrary", "arbitrary")),
        name="proj_xbc_conv",
    )(a, w, cw_g, cb_g)


def _split3_bf16(x):
    h1 = x.astype(BF16)
    r1 = x - h1.astype(F32)
    h2 = r1.astype(BF16)
    h3 = (r1 - h2.astype(F32)).astype(BF16)
    return h1, h2, h3


def _split2_bf16(x):
    hi = x.astype(BF16)
    return hi, (x - hi.astype(F32)).astype(BF16)


def _ssd_kernel(zs_ref, xbc_ref, dt_ref, dtb_ref, alog_ref, d_ref, ng_ref, ex_ref,
                y_ref, st_scr, cst_scr):
    q = SSD_CHUNK
    c_idx = pl.program_id(1)

    @pl.when(c_idx == 0)
    def _():
        st_scr[...] = jnp.zeros(st_scr.shape, F32)

    row = lax.broadcasted_iota(I32, (q, q), 0)
    col = lax.broadcasted_iota(I32, (q, q), 1)
    causal = row >= col
    tril16 = causal.astype(F32).astype(BF16)
    lo = lax.broadcasted_iota(I32, (q, LANES), 1) < SSD_HEAD_DIM

    dtr = dt_ref[...] + dtb_ref[...]
    dt_all = jnp.maximum(dtr, 0.0) + jnp.log1p(jnp.exp(-jnp.abs(dtr)))
    da = dt_all * (-jnp.exp(alog_ref[...]))
    h1, h2, h3 = _split3_bf16(da)
    cs_all = (jnp.dot(tril16, h1, preferred_element_type=F32)
              + jnp.dot(tril16, h2, preferred_element_type=F32)
              + jnp.dot(tril16, h3, preferred_element_type=F32))
    cst_scr[...] = cs_all.T
    last_all = cs_all[q - 1:q, :]
    ecs_all = jnp.exp(cs_all)
    wend_all = jnp.exp(last_all - cs_all)
    elast_all = jnp.exp(last_all)
    tok_splits = [_split2_bf16(v) for v in (dt_all, ecs_all, wend_all)]
    row_vals = jnp.concatenate(
        [elast_all, d_ref[...], jnp.zeros((SUBLANES - 2, LANES), F32)], axis=0)
    row_splits = _split3_bf16(row_vals)

    def group(g, carry):
        ex = ex_ref[g]
        spread = lambda parts: sum(jnp.dot(h, ex, preferred_element_type=F32) for h in parts)
        dt_x, ecs_x, wend_x = (spread(parts) for parts in tok_splits)
        rows_x = spread(row_splits)
        el = rows_x[0:1, :]
        dsk_x = rows_x[1:2, :]
        cs = pltpu.roll(cs_all, lax.rem(LANES - SSD_HPG * g, LANES), 1)
        cs_t = cst_scr[pl.ds(pl.multiple_of(g * SSD_HPG, SSD_HPG), SSD_HPG), :]

        xs = xbc_ref[g, :, 0:SSD_GW]
        bm = xbc_ref[g, :, SSD_GW:SSD_GW + SSD_D_STATE]
        cm = xbc_ref[g, :, SSD_GW + SSD_D_STATE:SSD_XBC_GW]

        bm16 = bm.astype(BF16)
        cm16 = cm.astype(BF16)
        cb = lax.dot_general(cm16, bm16, _NT, preferred_element_type=F32)
        st_t = st_scr[g]
        ystate = jnp.dot(cm16, st_t.astype(BF16), preferred_element_type=F32)

        xdt_g = xs * dt_x
        xw = (xdt_g * wend_x).astype(BF16)
        ys = []
        for p in range(SSD_HPG // 2):
            j0 = 2 * p
            xdt = xdt_g[:, p * LANES:(p + 1) * LANES]
            lmats = []
            for j in (j0, j0 + 1):
                diff = cs[:, j:j + 1] - cs_t[j:j + 1, :]
                dec = jnp.exp(jnp.where(causal, diff, -jnp.inf))
                lmats.append((cb * dec).astype(BF16))
            lmat = jnp.concatenate(lmats, axis=1)
            rhs = jnp.concatenate([jnp.where(lo, xdt, 0.0), jnp.where(lo, 0.0, xdt)],
                                  axis=0).astype(BF16)
            ys.append(jnp.dot(lmat, rhs, preferred_element_type=F32))
        y = jnp.concatenate(ys, axis=1) + ystate * ecs_x + dsk_x * xs
        st_scr[g] = st_t * el + jnp.dot(bm.T.astype(BF16), xw, preferred_element_type=F32)

        y = y * zs_ref[g]
        y = y * lax.rsqrt(jnp.mean(y * y, axis=-1, keepdims=True) + NORM_EPS) * ng_ref[g]
        y_ref[g] = y.astype(y_ref.dtype)
        return carry

    lax.fori_loop(0, SSD_N_GROUPS, group, 0)


def _head_spread_table():
    head = lax.broadcasted_iota(I32, (SSD_N_GROUPS, LANES, SSD_GW), 1)
    grp = lax.broadcasted_iota(I32, (SSD_N_GROUPS, LANES, SSD_GW), 0)
    chan = lax.broadcasted_iota(I32, (SSD_N_GROUPS, LANES, SSD_GW), 2)
    return (head == grp * SSD_HPG + chan // SSD_HEAD_DIM).astype(BF16)


def ssd_mixer(zs_g, xbc_g, small, dtb, alog, dskip, ng_g, *, bsz, s_len):
    g, q = SSD_N_GROUPS, SSD_CHUNK
    m = bsz * s_len
    nc = s_len // q

    def act(w):
        return pl.BlockSpec((g, q, w), lambda b, c: (0, b * nc + c, 0))

    def par(r, w):
        return pl.BlockSpec((g, r, w), lambda b, c: (0, 0, 0))

    head = pl.BlockSpec((1, LANES), lambda b, c: (0, 0))

    return pl.pallas_call(
        _ssd_kernel,
        grid=(bsz, nc),
        in_specs=[act(SSD_GW), act(SSD_XBC_GW),
                  pl.BlockSpec((q, LANES), lambda b, c: (b * nc + c, 1)),
                  head, head, head, par(1, SSD_GW), par(LANES, SSD_GW)],
        out_specs=act(SSD_GW),
        out_shape=jax.ShapeDtypeStruct((g, m, SSD_GW), BF16),
        scratch_shapes=[pltpu.VMEM((g, SSD_D_STATE, SSD_GW), F32),
                        pltpu.VMEM((LANES, q), F32)],
        compiler_params=_cparams(("arbitrary", "arbitrary")),
        name="ssd_mixer",
    )(zs_g, xbc_g, small, dtb, alog, dskip, ng_g, _head_spread_table())


def _dsa_kernel(q_ref, iq_ref, smq_ref, k_ref, v_ref, smk_ref, o_ref,
                key_scr, s_scr, mx_scr, l_scr, acc_scr, *, tq, ck, topk):
    i = pl.program_id(1)
    q0 = i * tq
    nch = lax.div(q0 + tq + ck - 1, ck)
    idx_scale = (IDX_HEAD_DIM ** -0.5) * (IDX_N_HEADS ** -0.5)
    att_scale = ATT_HEAD_DIM ** -0.5

    iw = smq_ref[0][:, IDX_HEAD_DIM:IDX_HEAD_DIM + IDX_N_HEADS]
    tpos = q0 + lax.broadcasted_iota(I32, (tq, ck), 0)
    lane_pos = lax.broadcasted_iota(I32, (tq, ck), 1)

    def score_chunk(c, carry):
        s0 = pl.multiple_of(c * ck, ck)
        ikc = smk_ref[0, pl.ds(s0, ck), :][:, :IDX_HEAD_DIM].astype(BF16)
        acc = jnp.zeros((tq, ck), F32)
        for h in range(IDX_N_HEADS):
            iqh = iq_ref[0, :, h * IDX_HEAD_DIM:(h + 1) * IDX_HEAD_DIM]
            d = lax.dot_general(iqh, ikc, _NT, preferred_element_type=F32)
            acc = acc + jnp.maximum(d, 0.0) * iw[:, h:h + 1]
        acc = acc * idx_scale
        bits = pltpu.bitcast(acc, I32)
        key = jnp.where(bits < 0, bits ^ jnp.int32(0x7FFFFFFF), bits)
        key = jnp.where(key == -1, 0, key)
        key_scr[c] = jnp.where(s0 + lane_pos <= tpos, key, jnp.int32(INT_MIN))
        return carry

    lax.fori_loop(0, nch, score_chunk, 0)

    def count_ge(cand):
        cand_b = jnp.broadcast_to(cand, (tq, LANES))

        def body(c, cnt):
            kc = key_scr[c]
            for u in range(ck // LANES):
                cnt = cnt + (kc[:, u * LANES:(u + 1) * LANES] >= cand_b).astype(I32)
            return cnt

        cnt = lax.fori_loop(0, nch, body, jnp.zeros((tq, LANES), I32))
        return jnp.sum(cnt.astype(F32), axis=1, keepdims=True)

    keep_all = q0 + lax.broadcasted_iota(I32, (tq, 1), 0) < topk

    def bis_cond(st):
        it, _, _, pending = st
        return jnp.logical_and(it < 32, pending > 0.0)

    def bis_step(st):
        it, thr, cnt_thr, _ = st
        cand = thr + lax.shift_left(jnp.int32(1), 31 - it)
        cnt = count_ge(cand)
        take = cnt >= float(topk)
        thr = jnp.where(take, cand, thr)
        cnt_thr = jnp.where(take, cnt, cnt_thr)
        done = jnp.logical_or(cnt_thr == float(topk), keep_all)
        return it + 1, thr, cnt_thr, jnp.sum(jnp.where(done, 0.0, 1.0))

    _, thr, cnt_thr, _ = lax.while_loop(
        bis_cond, bis_step,
        (jnp.int32(0), jnp.full((tq, 1), INT_MIN, I32), jnp.full((tq, 1), 2.0 * topk, F32),
         jnp.float32(1.0)))

    tied = jnp.logical_and(cnt_thr > float(topk), jnp.logical_not(keep_all))

    @pl.when(jnp.sum(jnp.where(tied, 1.0, 0.0)) > 0.0)
    def _():
        thr_t = jnp.broadcast_to(thr, (tq, LANES))
        lane = lax.broadcasted_iota(I32, (tq, LANES), 1)

        def count(pred):
            def body(c, cnt):
                kc = key_scr[c]
                for u in range(ck // LANES):
                    hit = pred(kc[:, u * LANES:(u + 1) * LANES], c * ck + u * LANES + lane)
                    cnt = cnt + hit.astype(I32)
                return cnt

            cnt = lax.fori_loop(0, nch, body, jnp.zeros((tq, LANES), I32))
            return jnp.sum(cnt.astype(F32), axis=1, keepdims=True)

        need = float(topk) - count(lambda kv, pos: kv > thr_t)
        nbits = (key_scr.shape[0] * ck).bit_length()

        def pos_step(it, below):
            cand = below + lax.shift_left(jnp.int32(1), nbits - 1 - it)
            cand_b = jnp.broadcast_to(cand, (tq, LANES))
            n = count(lambda kv, pos: jnp.logical_and(kv == thr_t, pos < cand_b))
            return jnp.where(n < need, cand, below)

        below = lax.fori_loop(0, nbits, pos_step, jnp.zeros((tq, 1), I32))
        first_dropped = jnp.where(tied, below + 1, jnp.int32(2 ** 30))
        drop_b = jnp.broadcast_to(first_dropped, (tq, LANES))

        def retire(c, carry):
            kc = key_scr[c]
            outs = []
            for u in range(ck // LANES):
                kv = kc[:, u * LANES:(u + 1) * LANES]
                drop = jnp.logical_and(kv == thr_t, c * ck + u * LANES + lane >= drop_b)
                outs.append(jnp.where(drop, jnp.int32(INT_MIN), kv))
            key_scr[c] = jnp.concatenate(outs, axis=1)
            return carry

        lax.fori_loop(0, nch, retire, 0)

    thr = jnp.where(keep_all, jnp.int32(INT_MIN + 1), jnp.maximum(thr, jnp.int32(INT_MIN + 1)))
    thr_b = jnp.broadcast_to(thr, (tq, LANES))

    rows = ATT_GROUP * tq
    exp2_scale = att_scale * math.log2(math.e)

    def stacked_q(g):
        return jnp.concatenate(
            [q_ref[0, :, (g * ATT_GROUP + hh) * ATT_HEAD_DIM:(g * ATT_GROUP + hh + 1) * ATT_HEAD_DIM]
             for hh in range(ATT_GROUP)], axis=0)

    def logits_pass(g, qg, chunks):
        ksl = slice(g * ATT_HEAD_DIM, (g + 1) * ATT_HEAD_DIM)
        mxs = [mx_scr[g % 2, hh * tq:(hh + 1) * tq, :] for hh in range(ATT_GROUP)]
        for c in chunks:
            s0 = pl.multiple_of(c * ck, ck)
            s = lax.dot_general(qg, k_ref[0, pl.ds(s0, ck), ksl], _NT,
                                preferred_element_type=F32)
            kc = key_scr[c]
            sel = [kc[:, u * LANES:(u + 1) * LANES] >= thr_b for u in range(ck // LANES)]
            for hh in range(ATT_GROUP):
                rsl = slice(hh * tq, (hh + 1) * tq)
                for u in range(ck // LANES):
                    lsl = slice(u * LANES, (u + 1) * LANES)
                    sm = jnp.where(sel[u], s[rsl, lsl], NEG_BIG)
                    s_scr[g % 2, c, rsl, lsl] = sm
                    mxs[hh] = jnp.maximum(mxs[hh], sm)
        for hh in range(ATT_GROUP):
            mx_scr[g % 2, hh * tq:(hh + 1) * tq, :] = mxs[hh]

    def pv_pass(g, m_b, chunks):
        ksl = slice(g * ATT_HEAD_DIM, (g + 1) * ATT_HEAD_DIM)
        lsum = l_scr[...]
        acc = acc_scr[...]
        for c in chunks:
            s0 = pl.multiple_of(c * ck, ck)
            ps = []
            for u in range(ck // LANES):
                p = jnp.exp2((s_scr[g % 2, c, :, u * LANES:(u + 1) * LANES] - m_b) * exp2_scale)
                lsum = lsum + p
                ps.append(p.astype(BF16))
            acc = acc + jnp.dot(jnp.concatenate(ps, axis=1), v_ref[0, pl.ds(s0, ck), ksl],
                                preferred_element_type=F32)
        l_scr[...] = lsum
        acc_scr[...] = acc

    def chunk_loop(body):
        def pair(t, carry):
            body((2 * t, 2 * t + 1))
            return carry

        lax.fori_loop(0, lax.shift_right_logical(nch, 1), pair, 0)

        @pl.when(lax.rem(nch, 2) == 1)
        def _():
            body((nch - 1,))

    mx_scr[...] = jnp.full(mx_scr.shape, NEG_BIG, F32)
    q_cur = stacked_q(0)
    chunk_loop(lambda chunks: logits_pass(0, q_cur, chunks))
    for g in range(ATT_N_KV):
        m_b = jnp.broadcast_to(jnp.max(mx_scr[g % 2], axis=1, keepdims=True), (rows, LANES))
        l_scr[...] = jnp.zeros(l_scr.shape, F32)
        acc_scr[...] = jnp.zeros(acc_scr.shape, F32)
        if g + 1 < ATT_N_KV:
            mx_scr[(g + 1) % 2] = jnp.full((rows, LANES), NEG_BIG, F32)
            q_nxt = stacked_q(g + 1)

            def fused(chunks, g=g, m_b=m_b, q_nxt=q_nxt):
                pv_pass(g, m_b, chunks)
                logits_pass(g + 1, q_nxt, chunks)
        else:
            def fused(chunks, g=g, m_b=m_b):
                pv_pass(g, m_b, chunks)

        chunk_loop(fused)
        o = acc_scr[...] / jnp.sum(l_scr[...], axis=1, keepdims=True)
        for hh in range(ATT_GROUP):
            h = g * ATT_GROUP + hh
            o_ref[0, :, h * ATT_HEAD_DIM:(h + 1) * ATT_HEAD_DIM] = (
                o[hh * tq:(hh + 1) * tq]).astype(o_ref.dtype)


def dsa_mixer(q, k, v, iq, small, *, bsz, s_len, tq=128, ck=512):
    topk = min(TOPK_MAX, s_len // 4)
    ck = min(ck, s_len)
    assert ck >= topk and s_len % ck == 0 and s_len % tq == 0
    nq = s_len // tq
    hq = ATT_N_HEADS * ATT_HEAD_DIM
    hk = ATT_N_KV * ATT_HEAD_DIM
    hi = IDX_N_HEADS * IDX_HEAD_DIM
    r3 = lambda a: a.reshape(bsz, s_len, a.shape[-1])
    out = pl.pallas_call(
        functools.partial(_dsa_kernel, tq=tq, ck=ck, topk=topk),
        grid=(bsz, nq),
        in_specs=[pl.BlockSpec((1, tq, hq), lambda b, i: (b, i, 0)),
                  pl.BlockSpec((1, tq, hi), lambda b, i: (b, i, 0)),
                  pl.BlockSpec((1, tq, LANES), lambda b, i: (b, i, 0)),
                  pl.BlockSpec((1, s_len, hk), lambda b, i: (b, 0, 0)),
                  pl.BlockSpec((1, s_len, hk), lambda b, i: (b, 0, 0)),
                  pl.BlockSpec((1, s_len, LANES), lambda b, i: (b, 0, 0))],
        out_specs=pl.BlockSpec((1, tq, hq), lambda b, i: (b, i, 0)),
        out_shape=jax.ShapeDtypeStruct((bsz, s_len, hq), BF16),
        scratch_shapes=[pltpu.VMEM((s_len // ck, tq, ck), I32),
                        pltpu.VMEM((2, s_len // ck, ATT_GROUP * tq, ck), F32),
                        pltpu.VMEM((2, ATT_GROUP * tq, LANES), F32),
                        pltpu.VMEM((ATT_GROUP * tq, LANES), F32),
                        pltpu.VMEM((ATT_GROUP * tq, ATT_HEAD_DIM), F32)],
        compiler_params=_cparams(("parallel", "arbitrary")),
        name="dsa_mixer",
    )(r3(q), r3(iq), r3(small), r3(k), r3(v), r3(small))
    return out.reshape(bsz * s_len, hq)


def _memattn_kernel(q_ref, kv_ref, o_ref):
    scale = MEM_HEAD_DIM ** -0.5
    for h in range(MEM_HEADS):
        sl = slice(h * MEM_HEAD_DIM, (h + 1) * MEM_HEAD_DIM)
        qh = q_ref[:, sl]
        kh = kv_ref[:, sl]
        vh = kv_ref[:, D_MODEL + h * MEM_HEAD_DIM:D_MODEL + (h + 1) * MEM_HEAD_DIM]
        s = lax.dot_general(qh, kh, _NT, preferred_element_type=F32) * scale
        p = jnp.exp(s - jnp.max(s, axis=1, keepdims=True))
        o = jnp.dot(p.astype(BF16), vh, preferred_element_type=F32)
        o_ref[:, sl] = (o / jnp.sum(p, axis=1, keepdims=True)).astype(o_ref.dtype)


def mem_attention(qm, kv, *, bsz, s_len, tq=512):
    m = bsz * s_len
    tq = min(tq, s_len)
    nq = s_len // tq
    n_mem = kv.shape[0] // bsz
    return pl.pallas_call(
        _memattn_kernel,
        grid=(bsz, nq),
        in_specs=[pl.BlockSpec((tq, D_MODEL), lambda b, i: (b * nq + i, 0)),
                  pl.BlockSpec((n_mem, 2 * D_MODEL), lambda b, i: (b, 0))],
        out_specs=pl.BlockSpec((tq, D_MODEL), lambda b, i: (b * nq + i, 0)),
        out_shape=jax.ShapeDtypeStruct((m, D_MODEL), BF16),
        compiler_params=_cparams(("parallel", "parallel")),
        name="mem_attention",
    )(qm, kv)


def _rope_tables(positions):
    m = positions.size
    pos = positions.astype(F32).reshape(m, 1)

    def unit(head_dim):
        rot = head_dim // ROPE_FRACTION
        half = rot // 2
        inv_freq = jnp.power(ROPE_THETA, -jnp.arange(half, dtype=F32) * 2.0 / rot)
        ang = pos * inv_freq
        cos, sin = jnp.cos(ang), jnp.sin(ang)
        one = jnp.ones((m, head_dim - rot), F32)
        zero = jnp.zeros((m, head_dim - rot), F32)
        zh = jnp.zeros((m, half), F32)
        c = jnp.concatenate([cos, cos, one], axis=1)
        s1 = jnp.concatenate([zh, sin, zero], axis=1)
        s2 = jnp.concatenate([-sin, zh, zero], axis=1)
        return c, s1, s2

    att = unit(ATT_HEAD_DIM)
    ic, is1, is2 = unit(IDX_HEAD_DIM)
    idx = tuple(jnp.concatenate([t, t], axis=1) for t in (ic, is1, is2))
    pad1 = jnp.ones((m, LANES - IDX_HEAD_DIM), F32)
    pad0 = jnp.zeros((m, LANES - IDX_HEAD_DIM), F32)
    small = (jnp.concatenate([ic, pad1], axis=1),
             jnp.concatenate([is1, pad0], axis=1),
             jnp.concatenate([is2, pad0], axis=1))
    return att, idx, small


def _prep_in_proj(w):
    offs = [0]
    for s in IN_PROJ_SIZES:
        offs.append(offs[-1] + s)
    seg = lambda i: w[:, offs[i]:offs[i + 1]]
    z, xbc, dt, q, k, v, iq, ik, iw, g_ssd, g_att = (seg(i) for i in range(11))
    d = w.shape[0]
    g = SSD_N_GROUPS
    xs_w = xbc[:, :SSD_D_INNER].reshape(d, g, SSD_GW)
    b_w = xbc[:, SSD_D_INNER:SSD_D_INNER + g * SSD_D_STATE].reshape(d, g, SSD_D_STATE)
    c_w = xbc[:, SSD_D_INNER + g * SSD_D_STATE:].reshape(d, g, SSD_D_STATE)
    xbc_gm = jnp.concatenate([xs_w, b_w, c_w], axis=2).reshape(d, g * SSD_XBC_GW)
    small = jnp.concatenate(
        [ik, iw, jnp.zeros((d, LANES - IDX_HEAD_DIM - IDX_N_HEADS), w.dtype),
         dt, jnp.zeros((d, LANES - SSD_N_HEADS), w.dtype)], axis=1)
    c16 = lambda a: a.astype(BF16)
    return dict(z=c16(z), xbc=c16(xbc_gm),
                q=c16(q), k=c16(k), v=c16(v), iq=c16(iq), small=c16(small),
                gates=c16(jnp.concatenate([g_ssd, g_att], axis=1)))


def _prep_ssd_params(conv_w, conv_b, dt_bias, a_log, d_skip, norm_g):
    g = SSD_N_GROUPS

    def gm_channels(a):
        r = a.shape[0]
        xs = a[:, :SSD_D_INNER].reshape(r, g, SSD_GW)
        b = a[:, SSD_D_INNER:SSD_D_INNER + g * SSD_D_STATE].reshape(r, g, SSD_D_STATE)
        c = a[:, SSD_D_INNER + g * SSD_D_STATE:].reshape(r, g, SSD_D_STATE)
        return jnp.transpose(jnp.concatenate([xs, b, c], axis=2), (1, 0, 2))

    def gm_heads(a):
        return jnp.pad(a.reshape(1, SSD_N_HEADS), ((0, 0), (0, LANES - SSD_N_HEADS)))

    return (gm_channels(conv_w), gm_channels(conv_b[None, :]), gm_heads(dt_bias),
            gm_heads(a_log), gm_heads(d_skip), norm_g.reshape(g, 1, SSD_GW))


def kernel(x, mem, positions, norm_mix_pre, norm_mix_post, w_in, ssd_conv_w, ssd_conv_b,
           ssd_dt_bias, ssd_a_log, ssd_d, ssd_norm, w_br_ssd, w_br_att, w_out,
           norm_mem_pre, norm_mem_post, mem_norm, w_mem_q, w_mem_kv, w_mem_o,
           norm_ffn_pre, norm_ffn_post, w_ffn_in, w_ffn_out):
    bsz, s_len, d = x.shape
    m = bsz * s_len
    depth = w_in.shape[0]
    x2 = x.reshape(m, d)
    mem2 = mem.reshape(bsz * mem.shape[1], d)
    rope_att, rope_idx, rope_small = _rope_tables(positions)
    rope_ex = lambda t: tuple((a, "m128") for a in t)
    tm = 1024
    gcol = D_MODEL // 1024

    gvec = lambda a: a.reshape(1, d)
    hn = rmsnorm_bf16(x2, norm_mix_pre[0])
    for l in range(depth):
        wp = _prep_in_proj(w_in[l])
        cw_g, cb_g, *ssd_par = _prep_ssd_params(ssd_conv_w[l], ssd_conv_b[l], ssd_dt_bias[l],
                                                 ssd_a_log[l], ssd_d[l], ssd_norm[l])
        zs_g = matmul(hn, wp["z"], tm=tm, tn=SSD_GW, out_gm=True, epi=_epi_silu, name="proj_z")
        xbc_g = xbc_conv_proj(hn, wp["xbc"], cw_g, cb_g, s_len=s_len, tm=tm)
        q = matmul(hn, wp["q"], tm=tm, tn=1024, out_dtype=BF16, name="proj_q",
                   epi=functools.partial(_epi_rope, ATT_HEAD_DIM // ROPE_FRACTION // 2, 1024 // LANES),
                   extras=rope_ex(rope_att))
        k = matmul(hn, wp["k"], tm=tm, tn=512, out_dtype=BF16, name="proj_k",
                   epi=functools.partial(_epi_rope, ATT_HEAD_DIM // ROPE_FRACTION // 2, 512 // LANES),
                   extras=rope_ex(rope_att))
        v = matmul(hn, wp["v"], tm=tm, tn=512, out_dtype=BF16, name="proj_v")
        iq = matmul(hn, wp["iq"], tm=tm, tn=1024, out_dtype=BF16, name="proj_iq",
                    epi=functools.partial(_epi_rope, IDX_HEAD_DIM // ROPE_FRACTION // 2, 1024 // LANES),
                    extras=rope_ex(rope_idx))
        small = matmul(hn, wp["small"], tm=tm, tn=2 * LANES, name="proj_small",
                       epi=functools.partial(_epi_rope, IDX_HEAD_DIM // ROPE_FRACTION // 2, 1),
                       extras=rope_ex(rope_small))
        gates = matmul(hn, wp["gates"], tm=tm, tn=1024, epi=_epi_sigmoid, name="proj_gates")

        y_g = ssd_mixer(zs_g, xbc_g, small, *ssd_par, bsz=bsz, s_len=s_len)
        att = dsa_mixer(q, k, v, iq, small, bsz=bsz, s_len=s_len)

        u_ssd = matmul(y_g, w_br_ssd[l].astype(BF16), tm=tm, tn=1024, a_gm="full",
                       epi=_epi_gate_mul, extras=((gates, "mn", 0),), name="branch_ssd")
        merged = matmul(att, w_br_att[l].astype(BF16), tm=tm, tn=1024, out_dtype=BF16,
                        epi=_epi_gate_mul_add, extras=((gates, "mn", gcol), (u_ssd, "mn", 0)),
                        name="branch_att_merge")
        x2, hn = matmul(merged, w_out[l].astype(BF16), tm=256, tn=D_MODEL, epi=_epi_norm_res_next,
                        extras=((gvec(norm_mix_post[l]), "n"), (x2, "mn"), (gvec(norm_mem_pre[l]), "n")),
                        out2_dtype=BF16, name="mix_out")

        qm = matmul(hn, w_mem_q[l].astype(BF16), tm=tm, tn=1024, out_dtype=BF16, name="mem_q")
        mem_n = rmsnorm_bf16(mem2, mem_norm[l])
        kvm = matmul(mem_n, w_mem_kv[l].astype(BF16), tm=tm, tn=1024, out_dtype=BF16, name="mem_kv")
        om = mem_attention(qm, kvm, bsz=bsz, s_len=s_len)
        x2, hn = matmul(om, w_mem_o[l].astype(BF16), tm=256, tn=D_MODEL, epi=_epi_norm_res_next,
                        extras=((gvec(norm_mem_post[l]), "n"), (x2, "mn"), (gvec(norm_ffn_pre[l]), "n")),
                        out2_dtype=BF16, name="mem_out")

        wf = w_ffn_in[l]
        hf = swiglu_in(hn, wf[:, :D_FF].astype(BF16), wf[:, D_FF:].astype(BF16), tm=tm, tn=512)
        w_fo = w_ffn_out[l].astype(BF16)
        if l + 1 < depth:
            x2, hn = matmul(hf, w_fo, tm=512, tn=D_MODEL, tk=D_FF // 4, epi=_epi_norm_res_next,
                            extras=((gvec(norm_ffn_post[l]), "n"), (x2, "mn"),
                                    (gvec(norm_mix_pre[l + 1]), "n")),
                            out2_dtype=BF16, name="ffn_out")
        else:
            x2 = matmul(hf, w_fo, tm=512, tn=D_MODEL, tk=D_FF // 4, epi=_epi_norm_res,
                        extras=((gvec(norm_ffn_post[l]), "n"), (x2, "mn")), name="ffn_out_last")

    return x2.reshape(bsz, s_len, d)
```
